```python
import jax, jax.numpy as jnp
from jax import lax
import numpy as np

D_MODEL = 1024
BATCH = 4
SEQ = 4096
DEPTH = 2

GRID_W = 64
CTX_LEN = 256
N_ADA = 6
HG_HEADS = 8
HG_DK = 128
HG_DV = 128
HG_CHUNK = 64
CONV_CH = 1024
CONV_K = 31
ATT_HEADS = 8
ATT_KV_HEADS = 4
ATT_GROUP = ATT_HEADS // ATT_KV_HEADS
HEAD_DIM = 128
ROPE_THETA = 10000.0
Q_BLOCK = 128
PEER_HEADS = 8
PEER_NKEYS = 128
PEER_EXPERTS = PEER_NKEYS * PEER_NKEYS
PEER_DQ = 256
PEER_TOPK = 16
PEER_BLOCK = 128

N_BRANCH = 3
EPS = 1e-6
ALPHA = (2 * DEPTH) ** 0.25
BETA = (8 * DEPTH) ** -0.25

IN_SPLITS = (
    ('hg_q', HG_HEADS * HG_DK),
    ('hg_f_fwd', HG_HEADS * HG_DK),
    ('hg_f_bwd', HG_HEADS * HG_DK),
    ('hg_i', HG_HEADS * HG_DV),
    ('hg_g', HG_HEADS * HG_DV),
    ('conv_glu', 2 * CONV_CH),
    ('att_q', ATT_HEADS * HEAD_DIM),
    ('att_k', ATT_KV_HEADS * HEAD_DIM),
    ('att_v', ATT_KV_HEADS * HEAD_DIM),
    ('merge_gate', N_BRANCH * D_MODEL),
)
IN_WIDTH = sum(w for _, w in IN_SPLITS)

kernel_name = 'hybrid_hgrn2_conformer_gqa_peer_dit'


def _split_in(z):
    out, off = {}, 0
    for name, w in IN_SPLITS:
        out[name] = z[..., off:off + w]
        off += w
    return out


def _layer_norm(x, g, b):
    xf = x.astype(jnp.float32)
    mu = jnp.mean(xf, -1, keepdims=True)
    var = jnp.mean(jnp.square(xf - mu), -1, keepdims=True)
    return ((xf - mu) * lax.rsqrt(var + EPS)).astype(x.dtype) * g + b


def _rms_norm(x, g):
    xf = x.astype(jnp.float32)
    return (xf * lax.rsqrt(jnp.mean(jnp.square(xf), -1, keepdims=True) + EPS)).astype(x.dtype) * g


def _modulate(x, shift, scale):
    return x * (1.0 + scale) + shift


def _hgrn_gates(z, lb):
    zf = z.astype(jnp.float32)
    f = lb + (1.0 - lb) * jax.nn.sigmoid(zf)
    log_f = jnp.log(jnp.maximum(f, jnp.finfo(jnp.float32).tiny))
    k = (1.0 - lb) * jax.nn.sigmoid(-zf)
    return log_f, k.astype(z.dtype)


def _hgrn_prep(parts, lb_fwd, lb_bwd):
    B, L = parts['hg_q'].shape[:2]
    heads = lambda a: a.reshape(B, L, HG_HEADS, -1)
    q = heads(jax.nn.silu(parts['hg_q']))
    v = heads(parts['hg_i'])
    lf_f, k_f = _hgrn_gates(parts['hg_f_fwd'], lb_fwd)
    lf_b, k_b = _hgrn_gates(parts['hg_f_bwd'], lb_bwd)
    return q, v, heads(lf_f), heads(k_f), heads(lf_b), heads(k_b)


def _hgrn_chunk_scan(q, k, v, log_f, s0):
    out_dtype = v.dtype
    B, L, H, _ = q.shape
    n = L // HG_CHUNK
    def chunks(a):
        a = a.astype(jnp.float32)
        return a.reshape(B, n, HG_CHUNK, H, a.shape[-1]).transpose(1, 0, 3, 2, 4)
    qc, kc, vc, fc = chunks(q), chunks(k), chunks(v), chunks(log_f)
    causal = jnp.tril(jnp.ones((HG_CHUNK, HG_CHUNK), bool))[:, :, None]
    def step(S, inp):
        qb, kb, vb, lf = inp
        b = jnp.cumsum(lf, axis=-2)
        diff = b[..., :, None, :] - b[..., None, :, :]
        decay = jnp.where(causal, jnp.exp(jnp.where(causal, diff, 0.0)), 0.0)
        scores = jnp.einsum('bhtd,bhtsd,bhsd->bhts', qb, decay, kb)
        o = jnp.einsum('bhts,bhsv->bhtv', scores, vb) + jnp.einsum('bhtd,bhdv->bhtv', qb * jnp.exp(b), S)
        b_last = b[..., -1, :]
        S_new = jnp.exp(b_last)[..., None] * S + jnp.einsum(
            'bhsd,bhsv->bhdv', kb * jnp.exp(b_last[..., None, :] - b), vb)
        return S_new, o
    s_fin, o = lax.scan(step, s0, (qc, kc, vc, fc))
    o = o.transpose(1, 0, 3, 2, 4).reshape(B, L, H, v.shape[-1])
    return o.astype(out_dtype), s_fin


def _hgrn_bidir(q, v, lf_f, k_f, lf_b, k_b, s_f, s_b):
    o_f, s_f = _hgrn_chunk_scan(q, k_f, v, lf_f, s_f)
    rev = lambda a: jnp.flip(a, axis=1)
    o_b, s_b = _hgrn_chunk_scan(rev(q), rev(k_b), rev(v), rev(lf_b), s_b)
    return o_f + rev(o_b), s_f, s_b


def _hgrn_out(o, gate, norm_g, w_o):
    B, L = o.shape[:2]
    o = _rms_norm(o, norm_g).reshape(B, L, HG_HEADS * HG_DV)
    return (o * jax.nn.silu(gate)) @ w_o


def _conformer_conv(glu_in, dw, db, ln_g, ln_b, w_o):
    a, g = jnp.split(glu_in, 2, axis=-1)
    u = a * jax.nn.sigmoid(g)
    u = lax.conv_general_dilated(
        u, dw[:, None, :], window_strides=(1,), padding=[(CONV_K // 2, CONV_K // 2)],
        dimension_numbers=('NWC', 'WIO', 'NWC'), feature_group_count=CONV_CH) + db
    u = jax.nn.silu(_layer_norm(u, ln_g, ln_b))
    return u @ w_o


def _grid_angles(L):
    n_rows = L // GRID_W
    row = jnp.repeat(jnp.arange(n_rows), GRID_W).astype(jnp.float32)
    col = jnp.tile(jnp.arange(GRID_W), n_rows).astype(jnp.float32)
    n_freq = HEAD_DIM // 4
    inv = ROPE_THETA ** (-jnp.arange(n_freq, dtype=jnp.float32) / n_freq)
    return jnp.concatenate([row[:, None] * inv, col[:, None] * inv], axis=-1)


def _rope_2d(x, ang):
    xp = x.reshape(*x.shape[:-1], HEAD_DIM // 2, 2)
    cos = jnp.cos(ang)[None, :, None, :].astype(x.dtype)
    sin = jnp.sin(ang)[None, :, None, :].astype(x.dtype)
    x0, x1 = xp[..., 0], xp[..., 1]
    return jnp.stack([x0 * cos - x1 * sin, x0 * sin + x1 * cos], axis=-1).reshape(x.shape)


def _attn_qkv(parts, qn_g, kn_g):
    B, L = parts['att_q'].shape[:2]
    q = _rms_norm(parts['att_q'].reshape(B, L, ATT_HEADS, HEAD_DIM), qn_g)
    k = _rms_norm(parts['att_k'].reshape(B, L, ATT_KV_HEADS, HEAD_DIM), kn_g)
    v = parts['att_v'].reshape(B, L, ATT_KV_HEADS, HEAD_DIM)
    return q, k, v


def _attend(q, k, v):
    B, Lq = q.shape[:2]
    qg = q.reshape(B, Lq, ATT_KV_HEADS, ATT_GROUP, HEAD_DIM)
    s = jnp.einsum('bqkgd,bskd->bkgqs', qg, k).astype(jnp.float32) * HEAD_DIM ** -0.5
    p = jax.nn.softmax(s, axis=-1).astype(v.dtype)
    return jnp.einsum('bkgqs,bskd->bqkgd', p, v).reshape(B, Lq, ATT_HEADS * HEAD_DIM)


def _attend_blocks(q, k, v):
    B, L = q.shape[:2]
    nb = L // Q_BLOCK
    qb = q.reshape(B, nb, Q_BLOCK, ATT_HEADS, HEAD_DIM).transpose(1, 0, 2, 3, 4)
    o = lax.map(lambda qq: _attend(qq, k, v), qb)
    return o.transpose(1, 0, 2, 3).reshape(B, L, ATT_HEADS * HEAD_DIM)


def _merge(gate_logits, b_hg, b_conv, b_att, w_out):
    B, L = gate_logits.shape[:2]
    g = jax.nn.sigmoid(gate_logits).reshape(B, L, N_BRANCH, D_MODEL)
    return (g[:, :, 0] * b_hg + g[:, :, 1] * b_conv + g[:, :, 2] * b_att) @ w_out


def _peer(h, wq, k1, k2, u_tab, v_tab):
    shape = h.shape
    hb = h.reshape(-1, PEER_BLOCK, D_MODEL)
    def block(t):
        q = (t @ wq).reshape(PEER_BLOCK, PEER_HEADS, 2, PEER_DQ // 2)
        s1 = jnp.einsum('thd,nd->thn', q[:, :, 0], k1).astype(jnp.float32)
        s2 = jnp.einsum('thd,nd->thn', q[:, :, 1], k2).astype(jnp.float32)
        v1, i1 = lax.top_k(s1, PEER_TOPK)
        v2, i2 = lax.top_k(s2, PEER_TOPK)
        cand = (v1[..., :, None] + v2[..., None, :]).reshape(PEER_BLOCK, PEER_HEADS, PEER_TOPK * PEER_TOPK)
        score, ci = lax.top_k(cand, PEER_TOPK)
        expert = (jnp.take_along_axis(i1, ci // PEER_TOPK, axis=-1) * PEER_NKEYS
                  + jnp.take_along_axis(i2, ci % PEER_TOPK, axis=-1))
        g = jax.nn.softmax(score, axis=-1).astype(t.dtype)
        act = jax.nn.gelu(jnp.einsum('thkd,td->thk', u_tab[expert], t))
        return jnp.einsum('thk,thkd->td', g * act, v_tab[expert]).astype(t.dtype)
    return lax.map(block, hb).reshape(shape)


def setup_inputs(seed: int = 0) -> dict:
    key = jax.random.key(seed)
    ks = iter(jax.random.split(key, 28))
    D = D_MODEL
    def nrm(shape, scale):
        return jax.random.normal(next(ks), shape, jnp.float32) * scale
    return {
        'x': nrm((BATCH, SEQ, D), 1.0),
        'c': nrm((BATCH, D), 1.0),
        'ctx': nrm((BATCH, CTX_LEN, D), 1.0),
        'c_ctx': nrm((D,), 1.0),
        'w_ada': nrm((DEPTH, D, N_ADA * D), D ** -0.5),
        'b_ada': nrm((DEPTH, N_ADA * D), 0.02),
        'w_in': nrm((DEPTH, D, IN_WIDTH), D ** -0.5),
        'hg_lb_logits': nrm((DEPTH, 2, HG_HEADS * HG_DK), 1.0),
        'hg_norm_g': 1.0 + nrm((DEPTH, HG_DV), 0.02),
        'w_hg_o': nrm((DEPTH, HG_HEADS * HG_DV, D), (HG_HEADS * HG_DV) ** -0.5),
        'conv_dw': nrm((DEPTH, CONV_K, CONV_CH), CONV_K ** -0.5),
        'conv_b': nrm((DEPTH, CONV_CH), 0.02),
        'conv_ln_g': 1.0 + nrm((DEPTH, CONV_CH), 0.02),
        'conv_ln_b': nrm((DEPTH, CONV_CH), 0.02),
        'w_conv_o': nrm((DEPTH, CONV_CH, D), CONV_CH ** -0.5),
        'att_qn_g': 1.0 + nrm((DEPTH, HEAD_DIM), 0.02),
        'att_kn_g': 1.0 + nrm((DEPTH, HEAD_DIM), 0.02),
        'w_att_o': nrm((DEPTH, ATT_HEADS * HEAD_DIM, D), (ATT_HEADS * HEAD_DIM) ** -0.5),
        'w_out': nrm((DEPTH, D, D), D ** -0.5 * BETA),
        'ln1_g': 1.0 + nrm((DEPTH, D), 0.02),
        'ln1_b': nrm((DEPTH, D), 0.02),
        'peer_wq': nrm((DEPTH, D, PEER_HEADS * PEER_DQ), D ** -0.5),
        'peer_k1': nrm((DEPTH, PEER_NKEYS, PEER_DQ // 2), (PEER_DQ // 2) ** -0.5),
        'peer_k2': nrm((DEPTH, PEER_NKEYS, PEER_DQ // 2), (PEER_DQ // 2) ** -0.5),
        'peer_u': nrm((DEPTH, PEER_EXPERTS, D), D ** -0.5),
        'peer_v': nrm((DEPTH, PEER_EXPERTS, D), BETA * PEER_HEADS ** -0.5),
        'ln2_g': 1.0 + nrm((DEPTH, D), 0.02),
        'ln2_b': nrm((DEPTH, D), 0.02),
    }


def reference(x, c, ctx, c_ctx, w_ada, b_ada, w_in, hg_lb_logits, hg_norm_g, w_hg_o,
              conv_dw, conv_b, conv_ln_g, conv_ln_b, w_conv_o, att_qn_g, att_kn_g, w_att_o,
              w_out, ln1_g, ln1_b, peer_wq, peer_k1, peer_k2, peer_u, peer_v, ln2_g, ln2_b):
    B, L, _ = x.shape
    ang = _grid_angles(L)
    lb_p = jax.nn.softmax(hg_lb_logits.astype(jnp.float32), axis=0)
    lb = jnp.cumsum(lb_p, axis=0) - lb_p
    sil_c = jax.nn.silu(c)
    sil_cc = jax.nn.silu(c_ctx)
    for l in range(DEPTH):
        last = l == DEPTH - 1
        mx = (sil_c @ w_ada[l] + b_ada[l])[:, None, :]
        mc = sil_cc @ w_ada[l] + b_ada[l]
        xsh1, xsc1, xg1, xsh2, xsc2, xg2 = jnp.split(mx, N_ADA, axis=-1)
        csh1, csc1, cg1, csh2, csc2, cg2 = jnp.split(mc, N_ADA, axis=-1)
        px = _split_in(_modulate(x, xsh1, xsc1) @ w_in[l])
        pc = _split_in(_modulate(ctx, csh1, csc1) @ w_in[l])

        zero = jnp.zeros((B, HG_HEADS, HG_DK, HG_DV), jnp.float32)
        oc_hg, st_f, st_b = _hgrn_bidir(*_hgrn_prep(pc, lb[l, 0], lb[l, 1]), zero, zero)
        ox_hg, _, _ = _hgrn_bidir(*_hgrn_prep(px, lb[l, 0], lb[l, 1]), st_f, st_b)

        qx, kx, vx = _attn_qkv(px, att_qn_g[l], att_kn_g[l])
        qc, kc, vc = _attn_qkv(pc, att_qn_g[l], att_kn_g[l])
        qx, kx = _rope_2d(qx, ang), _rope_2d(kx, ang)
        ox_att = _attend_blocks(qx, jnp.concatenate([kx, kc], axis=1), jnp.concatenate([vx, vc], axis=1))

        mix_x = _merge(px['merge_gate'],
                       _hgrn_out(ox_hg, px['hg_g'], hg_norm_g[l], w_hg_o[l]),
                       _conformer_conv(px['conv_glu'], conv_dw[l], conv_b[l], conv_ln_g[l], conv_ln_b[l], w_conv_o[l]),
                       ox_att @ w_att_o[l], w_out[l])
        x = _layer_norm(ALPHA * x + xg1 * mix_x, ln1_g[l], ln1_b[l])
        x = _layer_norm(ALPHA * x + xg2 * _peer(_modulate(x, xsh2, xsc2), peer_wq[l], peer_k1[l],
                                                  peer_k2[l], peer_u[l], peer_v[l]), ln2_g[l], ln2_b[l])

        if not last:
            mix_c = _merge(pc['merge_gate'],
                           _hgrn_out(oc_hg, pc['hg_g'], hg_norm_g[l], w_hg_o[l]),
                           _conformer_conv(pc['conv_glu'], conv_dw[l], conv_b[l], conv_ln_g[l], conv_ln_b[l], w_conv_o[l]),
                           _attend(qc, kc, vc) @ w_att_o[l], w_out[l])
            ctx = _layer_norm(ALPHA * ctx + cg1 * mix_c, ln1_g[l], ln1_b[l])
            ctx = _layer_norm(ALPHA * ctx + cg2 * _peer(_modulate(ctx, csh2, csc2), peer_wq[l], peer_k1[l],
                                                          peer_k2[l], peer_u[l], peer_v[l]), ln2_g[l], ln2_b[l])
    return x
```

```python
import functools
import math

import numpy as np
import jax
import jax.numpy as jnp
from jax import lax
from jax.experimental import pallas as pl
from jax.experimental.pallas import tpu as pltpu

F32 = jnp.float32
BF16 = jnp.bfloat16

D_MODEL = 1024
N_ADA = 6
HG_HEADS = 8
HG_D = 128
CONV_K = 31
ATT_HEADS = 8
ATT_KV_HEADS = 4
ATT_GROUP = ATT_HEADS // ATT_KV_HEADS
HEAD_DIM = 128
GRID_W = 64
ROPE_THETA = 10000.0
PEER_HEADS = 8
PEER_NKEYS = 128
PEER_TOPK = 16
EPS = 1e-6
F32_TINY = float(np.finfo(np.float32).tiny)

LANES = 128
SUBLANES = 8
VMEM_LIMIT_BYTES = 52 * 1024 * 1024

COL_HG_Q, COL_HG_FF, COL_HG_FB, COL_HG_I, COL_HG_G = 0, 1, 2, 3, 4
COL_CONV_A, COL_CONV_G, COL_ATT_Q, COL_ATT_KV = 5, 6, 7, 8
COL_GATE0 = 9
IN_WIDTH = 12 * D_MODEL


def _params(*sem):
    return pltpu.CompilerParams(dimension_semantics=sem, vmem_limit_bytes=VMEM_LIMIT_BYTES)


def _dot(a, b):
    return jnp.dot(a, b, preferred_element_type=F32)


def _dot_nt(a, b):
    return lax.dot_general(a, b, (((1,), (1,)), ((), ())), preferred_element_type=F32)


def _dot_tn(a, b):
    return lax.dot_general(a, b, (((0,), (0,)), ((), ())), preferred_element_type=F32)


def _sigmoid(x):
    return 1.0 / (1.0 + jnp.exp(-x))


def _silu(x):
    return x * _sigmoid(x)


def _layer_norm(y, g, b):
    mu = jnp.mean(y, axis=-1, keepdims=True)
    d = y - mu
    var = jnp.mean(d * d, axis=-1, keepdims=True)
    return d * lax.rsqrt(var + EPS) * g + b


def _ada_kernel(c_ref, w_ref, b_ref, o_ref):
    s = _silu(c_ref[...]).astype(BF16)
    o_ref[...] = _dot(s, w_ref[...].astype(BF16)) + b_ref[...]


def ada_mod(cc, w, b):
    rows = cc.shape[0]
    tn = 1024
    return pl.pallas_call(
        _ada_kernel,
        grid=(w.shape[1] // tn,),
        in_specs=[
            pl.BlockSpec((rows, D_MODEL), lambda j: (0, 0)),
            pl.BlockSpec((D_MODEL, tn), lambda j: (0, j)),
            pl.BlockSpec((1, tn), lambda j: (0, j)),
        ],
        out_specs=pl.BlockSpec((rows, tn), lambda j: (0, j)),
        out_shape=jax.ShapeDtypeStruct((rows, w.shape[1]), F32),
        compiler_params=_params("parallel"),
        name="ada_mod",
    )(cc, w, b.reshape(1, -1))


def _inproj_kernel(x_ref, sh_ref, sc_ref, w_ref, o_ref, xm_ref):
    @pl.when(pl.program_id(1) == 0)
    def _():
        xm_ref[...] = (x_ref[...] * (1.0 + sc_ref[...]) + sh_ref[...]).astype(BF16)

    o_ref[...] = _dot(xm_ref[...], w_ref[...])


def in_proj(x2, sh, sc, w_bf16, rows_per_group):
    n = x2.shape[0]
    tm = min(1024, rows_per_group)
    tn = 1024
    grp = lambda i, j: ((i * tm) // rows_per_group, 0, 0)
    return pl.pallas_call(
        _inproj_kernel,
        grid=(n // tm, IN_WIDTH // tn),
        in_specs=[
            pl.BlockSpec((tm, D_MODEL), lambda i, j: (i, 0)),
            pl.BlockSpec((None, 1, D_MODEL), grp),
            pl.BlockSpec((None, 1, D_MODEL), grp),
            pl.BlockSpec((D_MODEL, tn), lambda i, j: (0, j)),
        ],
        out_specs=pl.BlockSpec((tm, tn), lambda i, j: (i, j)),
        out_shape=jax.ShapeDtypeStruct((n, IN_WIDTH), F32),
        scratch_shapes=[pltpu.VMEM((tm, D_MODEL), BF16)],
        compiler_params=_params("parallel", "arbitrary"),
        name="in_proj",
    )(x2, sh, sc, w_bf16)


def _hgrn_constants(chunk, reverse):
    c = chunk
    levels = []
    m = c // 2
    while m >= 1:
        levels.append(m)
        m //= 2
    t = np.arange(c)[:, None]
    u = np.arange(c)[None, :]
    mats = [(u <= t), (u > t)]
    masks = [np.eye(c, dtype=bool)]
    for m in levels:
        blk = t // (2 * m)
        r = blk * 2 * m + m
        upper = (t % (2 * m)) >= m
        a = np.where(upper, (u > r) & (u <= t), (u > t) & (u <= r))
        mats.append(a)
        ts = np.arange(c)[:, None]
        ss = np.arange(c)[None, :]
        same = (ts // (2 * m)) == (ss // (2 * m))
        masks.append(same & ((ts % (2 * m)) >= m) & ((ss % (2 * m)) < m))
    amat = np.concatenate([np.asarray(a, np.float32) for a in mats], axis=0)
    msk = np.stack([np.asarray(a, np.float32) for a in masks], axis=0)
    if reverse:
        amat = amat.reshape(len(mats), c, c)[:, ::-1, ::-1].reshape(len(mats) * c, c)
        msk = msk[:, ::-1, ::-1]
    return jnp.asarray(amat, BF16), jnp.asarray(np.ascontiguousarray(msk), F32), len(levels)


def _hgrn_direction(q, z, v, lb, amat, mask_ref, st, chunk, n_levels, last_row):
    c = chunk
    f = lb + (1.0 - lb) * _sigmoid(z)
    lf = jnp.log(jnp.maximum(f, F32_TINY))
    kk = (1.0 - lb) * _sigmoid(-z)
    hi = lf.astype(BF16)
    lo = (lf - hi.astype(F32)).astype(BF16)
    g = _dot(amat, jnp.concatenate([hi, lo], axis=1))
    dall = g[:, :HG_D] + g[:, HG_D:]
    bq = dall[0:c]
    bk = dall[c:2 * c]
    vb = v.astype(BF16)
    scores = _dot_nt(q.astype(BF16), kk.astype(BF16)) * mask_ref[0]
    for m in range(n_levels):
        e = jnp.exp(dall[(2 + m) * c:(3 + m) * c])
        scores = scores + _dot_nt((q * e).astype(BF16), (kk * e).astype(BF16)) * mask_ref[m + 1]
    qd = (q * jnp.exp(bq)).astype(BF16)
    kd = (kk * jnp.exp(bk)).astype(BF16)
    o = _dot(scores.astype(BF16), vb) + _dot_nt(qd, st.astype(BF16))
    st_new = st * jnp.exp(bq[last_row:last_row + 1]) + _dot_tn(vb, kd)
    return o, st_new


def _hgrn_kernel(qf_ref, zf_ref, vf_ref, qb_ref, zb_ref, vb_ref, lbf_ref, lbb_ref,
                 af_ref, mf_ref, ab_ref, mb_ref, s0f_ref, s0b_ref,
                 of_ref, ob_ref, sf_ref, sb_ref, *, chunk, n_levels, heads):
    @pl.when(pl.program_id(2) == 0)
    def _():
        sf_ref[...] = s0f_ref[...]
        sb_ref[...] = s0b_ref[...]

    af = af_ref[...]
    ab = ab_ref[...]
    for h in range(heads):
        sl = slice(h * HG_D, (h + 1) * HG_D)
        o, st = _hgrn_direction(_silu(qf_ref[:, sl]), zf_ref[:, sl], vf_ref[:, sl], lbf_ref[:, sl],
                                af, mf_ref, sf_ref[h], chunk, n_levels, chunk - 1)
        of_ref[:, sl] = o
        sf_ref[h] = st
        o, st = _hgrn_direction(_silu(qb_ref[:, sl]), zb_ref[:, sl], vb_ref[:, sl], lbb_ref[:, sl],
                                ab, mb_ref, sb_ref[h], chunk, n_levels, 0)
        ob_ref[:, sl] = o
        sb_ref[h] = st


def hgrn_scan(z, lb_f, lb_b, s0f, s0b, batch, seq, chunk=128, heads=2):
    n = batch * seq
    nb = seq // chunk
    hw = heads * HG_D
    cpb = D_MODEL // hw
    af, mf, n_levels = _hgrn_constants(chunk, False)
    ab, mb, _ = _hgrn_constants(chunk, True)
    fwd = lambda seg: pl.BlockSpec((chunk, hw), lambda b, h, c: (b * nb + c, seg * cpb + h))
    bwd = lambda seg: pl.BlockSpec((chunk, hw), lambda b, h, c: (b * nb + nb - 1 - c, seg * cpb + h))
    const2 = lambda a: pl.BlockSpec(a.shape, lambda b, h, c: (0, 0))
    const3 = lambda a: pl.BlockSpec(a.shape, lambda b, h, c: (0, 0, 0))
    state = pl.BlockSpec((None, heads, HG_D, HG_D), lambda b, h, c: (b, h, 0, 0))
    lbspec = pl.BlockSpec((1, hw), lambda b, h, c: (0, h))
    kern = functools.partial(_hgrn_kernel, chunk=chunk, n_levels=n_levels, heads=heads)
    return pl.pallas_call(
        kern,
        grid=(batch, HG_HEADS // heads, nb),
        in_specs=[fwd(COL_HG_Q), fwd(COL_HG_FF), fwd(COL_HG_I),
                  bwd(COL_HG_Q), bwd(COL_HG_FB), bwd(COL_HG_I),
                  lbspec, lbspec, const2(af), const3(mf), const2(ab), const3(mb), state, state],
        out_specs=[pl.BlockSpec((chunk, hw), lambda b, h, c: (b * nb + c, h)),
                   pl.BlockSpec((chunk, hw), lambda b, h, c: (b * nb + nb - 1 - c, h)),
                   state, state],
        out_shape=[jax.ShapeDtypeStruct((n, D_MODEL), F32), jax.ShapeDtypeStruct((n, D_MODEL), F32),
                   jax.ShapeDtypeStruct(s0f.shape, F32), jax.ShapeDtypeStruct(s0b.shape, F32)],
        compiler_params=_params("parallel", "parallel", "arbitrary"),
        name="hgrn_scan",
    )(z, z, z, z, z, z, lb_f, lb_b, af, mf, ab, mb, s0f, s0b)


CONV_HALO = 16


def _conv_kernel(a_ref, g_ref, ap_ref, gp_ref, an_ref, gn_ref, dw_ref, db_ref, lg_ref, lb_ref,
                 o_ref, u_ref, *, tl):
    i = pl.program_id(1)
    nblk = pl.num_programs(1)
    glu = lambda a, g: a * _sigmoid(g)
    prev_ok = (i > 0).astype(F32)
    next_ok = (i < nblk - 1).astype(F32)
    u_ref[0:CONV_HALO, :] = glu(ap_ref[...], gp_ref[...]) * prev_ok
    u_ref[CONV_HALO:CONV_HALO + tl, :] = glu(a_ref[...], g_ref[...])
    u_ref[CONV_HALO + tl:, :] = glu(an_ref[...], gn_ref[...]) * next_ok
    off = CONV_HALO - CONV_K // 2
    acc = jnp.zeros((tl, D_MODEL), F32) + db_ref[...]
    for k in range(CONV_K):
        acc = acc + u_ref[off + k:off + k + tl, :] * dw_ref[k:k + 1, :]
    y = _layer_norm(acc, lg_ref[...], lb_ref[...])
    o_ref[...] = _silu(y).astype(BF16)


def conformer_conv(z, dw, db, ln_g, ln_b, batch, seq):
    n = batch * seq
    tl = min(256, seq)
    nblk = seq // tl
    hb = tl // CONV_HALO
    hpb = seq // CONV_HALO
    cur = lambda col: pl.BlockSpec((tl, D_MODEL), lambda b, i: (b * nblk + i, col))
    prev = lambda col: pl.BlockSpec(
        (CONV_HALO, D_MODEL), lambda b, i: (b * hpb + jnp.maximum(i * hb - 1, 0), col))
    nxt = lambda col: pl.BlockSpec(
        (CONV_HALO, D_MODEL), lambda b, i: (b * hpb + jnp.minimum((i + 1) * hb, hpb - 1), col))
    vec = lambda r: pl.BlockSpec((r, D_MODEL), lambda b, i: (0, 0))
    return pl.pallas_call(
        functools.partial(_conv_kernel, tl=tl),
        grid=(batch, nblk),
        in_specs=[cur(COL_CONV_A), cur(COL_CONV_G), prev(COL_CONV_A), prev(COL_CONV_G),
                  nxt(COL_CONV_A), nxt(COL_CONV_G), vec(CONV_K), vec(1), vec(1), vec(1)],
        out_specs=pl.BlockSpec((tl, D_MODEL), lambda b, i: (b * nblk + i, 0)),
        out_shape=jax.ShapeDtypeStruct((n, D_MODEL), BF16),
        scratch_shapes=[pltpu.VMEM((tl + 2 * CONV_HALO, D_MODEL), F32)],
        compiler_params=_params("parallel", "parallel"),
        name="conformer_conv",
    )(z, z, z, z, z, z, dw, db.reshape(1, -1), ln_g.reshape(1, -1), ln_b.reshape(1, -1))


def _rope_tables(seq):
    n_rows = seq // GRID_W
    row = jnp.repeat(jnp.arange(n_rows), GRID_W).astype(F32)
    col = jnp.tile(jnp.arange(GRID_W), n_rows).astype(F32)
    n_freq = HEAD_DIM // 4
    inv = ROPE_THETA ** (-jnp.arange(n_freq, dtype=F32) / n_freq)
    ang = jnp.concatenate([row[:, None] * inv, col[:, None] * inv], axis=-1)
    cos = jnp.repeat(jnp.cos(ang), 2, axis=-1)
    sin = jnp.repeat(jnp.sin(ang), 2, axis=-1)
    sign = jnp.tile(jnp.asarray([-1.0, 1.0], F32), HEAD_DIM // 2)
    return cos, sin * sign


def _qkv_kernel(q_ref, kv_ref, qg_ref, kg_ref, cos_ref, sin_ref, qo_ref, ko_ref, vo_ref, *, rope):
    lane = lax.broadcasted_iota(jnp.int32, (1, HEAD_DIM), 1)
    even = (lane % 2) == 0

    def norm_rot(x, g):
        y = x * lax.rsqrt(jnp.mean(x * x, axis=-1, keepdims=True) + EPS) * g
        if rope:
            partner = jnp.where(even, pltpu.roll(y, HEAD_DIM - 1, 1), pltpu.roll(y, 1, 1))
            y = y * cos_ref[...] + partner * sin_ref[...]
        return y

    for h in range(ATT_HEADS):
        x = q_ref[:, h * HEAD_DIM:(h + 1) * HEAD_DIM]
        qo_ref[h] = (norm_rot(x, qg_ref[...]) * (HEAD_DIM ** -0.5)).astype(BF16)
    for h in range(ATT_KV_HEADS):
        x = kv_ref[:, h * HEAD_DIM:(h + 1) * HEAD_DIM]
        ko_ref[h] = norm_rot(x, kg_ref[...]).astype(BF16)
        off = (ATT_KV_HEADS + h) * HEAD_DIM
        vo_ref[h] = kv_ref[:, off:off + HEAD_DIM].astype(BF16)


def attn_qkv(z, qn_g, kn_g, cos, sin, batch, seq, rope):
    tl = min(512, seq)
    nblk = seq // tl
    tab = pl.BlockSpec((tl, HEAD_DIM), lambda b, i: (i, 0))
    vec = pl.BlockSpec((1, HEAD_DIM), lambda b, i: (0, 0))
    out = lambda nh: pl.BlockSpec((None, nh, tl, HEAD_DIM), lambda b, i: (b, 0, i, 0))
    return pl.pallas_call(
        functools.partial(_qkv_kernel, rope=rope),
        grid=(batch, nblk),
        in_specs=[pl.BlockSpec((tl, D_MODEL), lambda b, i: (b * nblk + i, COL_ATT_Q)),
                  pl.BlockSpec((tl, D_MODEL), lambda b, i: (b * nblk + i, COL_ATT_KV)),
                  vec, vec, tab, tab],
        out_specs=[out(ATT_HEADS), out(ATT_KV_HEADS), out(ATT_KV_HEADS)],
        out_shape=[jax.ShapeDtypeStruct((batch, ATT_HEADS, seq, HEAD_DIM), BF16),
                   jax.ShapeDtypeStruct((batch, ATT_KV_HEADS, seq, HEAD_DIM), BF16),
                   jax.ShapeDtypeStruct((batch, ATT_KV_HEADS, seq, HEAD_DIM), BF16)],
        compiler_params=_params("parallel", "parallel"),
        name="attn_qkv",
    )(z, z, qn_g.reshape(1, -1), kn_g.reshape(1, -1), cos, sin)


def _attn_kernel(q_ref, k_ref, v_ref, o_ref, *, tq):
    q = q_ref[...].reshape(ATT_GROUP * tq, HEAD_DIM)
    s = _dot_nt(q, k_ref[...])
    m = jnp.max(s, axis=-1, keepdims=True)
    p = jnp.exp(s - m)
    l = jnp.sum(p, axis=-1, keepdims=True)
    o = _dot(p.astype(BF16), v_ref[...]) / l
    for g in range(ATT_GROUP):
        o_ref[:, g * HEAD_DIM:(g + 1) * HEAD_DIM] = o[g * tq:(g + 1) * tq].astype(BF16)


def attention(q, k, v):
    batch, _, lq, _ = q.shape
    s_len = k.shape[2]
    tq = min(128, lq)
    nq = lq // tq
    kv = pl.BlockSpec((None, None, s_len, HEAD_DIM), lambda b, kh, i: (b, kh, 0, 0))
    return pl.pallas_call(
        functools.partial(_attn_kernel, tq=tq),
        grid=(batch, ATT_KV_HEADS, nq),
        in_specs=[pl.BlockSpec((None, ATT_GROUP, tq, HEAD_DIM), lambda b, kh, i: (b, kh, i, 0)), kv, kv],
        out_specs=pl.BlockSpec((tq, ATT_GROUP * HEAD_DIM), lambda b, kh, i: (b * nq + i, kh)),
        out_shape=jax.ShapeDtypeStruct((batch * lq, D_MODEL), BF16),
        compiler_params=_params("parallel", "parallel", "arbitrary"),
        name="attention",
    )(q, k, v)


def _merge_kernel(of_ref, ob_ref, hg_ref, uc_ref, at_ref, g0_ref, g1_ref, g2_ref, x_ref, xg_ref,
                  ng_ref, whg_ref, wcv_ref, wat_ref, wout_ref, lg_ref, lb_ref, o_ref, *, alpha):
    o = of_ref[...] + ob_ref[...]
    parts = []
    for h in range(HG_HEADS):
        oh = o[:, h * HG_D:(h + 1) * HG_D]
        parts.append(oh * lax.rsqrt(jnp.mean(oh * oh, axis=-1, keepdims=True) + EPS) * ng_ref[...])
    on = jnp.concatenate(parts, axis=1) * _silu(hg_ref[...])
    b_hg = _dot(on.astype(BF16), whg_ref[...])
    b_cv = _dot(uc_ref[...], wcv_ref[...])
    b_at = _dot(at_ref[...], wat_ref[...])
    mix_in = _sigmoid(g0_ref[...]) * b_hg + _sigmoid(g1_ref[...]) * b_cv + _sigmoid(g2_ref[...]) * b_at
    mix = _dot(mix_in.astype(BF16), wout_ref[...])
    y = alpha * x_ref[...] + xg_ref[...] * mix
    o_ref[...] = _layer_norm(y, lg_ref[...], lb_ref[...])


def merge_ln(o_f, o_b, z, uc, att, x2, xg, norm_g, w_hg, w_cv, w_at, w_out, ln_g, ln_b,
             rows_per_group, alpha):
    n = x2.shape[0]
    tm = min(256, rows_per_group)
    row = lambda col: pl.BlockSpec((tm, D_MODEL), lambda i: (i, col))
    wspec = pl.BlockSpec((D_MODEL, D_MODEL), lambda i: (0, 0))
    vec = lambda w: pl.BlockSpec((1, w), lambda i: (0, 0))
    grp = pl.BlockSpec((None, 1, D_MODEL), lambda i: ((i * tm) // rows_per_group, 0, 0))
    return pl.pallas_call(
        functools.partial(_merge_kernel, alpha=alpha),
        grid=(n // tm,),
        in_specs=[row(0), row(0), row(COL_HG_G), row(0), row(0),
                  row(COL_GATE0), row(COL_GATE0 + 1), row(COL_GATE0 + 2), row(0), grp,
                  vec(HG_D), wspec, wspec, wspec, wspec, vec(D_MODEL), vec(D_MODEL)],
        out_specs=row(0),
        out_shape=jax.ShapeDtypeStruct((n, D_MODEL), F32),
        compiler_params=_params("parallel"),
        name="merge_ln",
    )(o_f, o_b, z, uc, att, z, z, z, x2, xg, norm_g.reshape(1, -1),
      w_hg, w_cv, w_at, w_out, ln_g.reshape(1, -1), ln_b.reshape(1, -1))


def _peer_query_kernel(x_ref, sh_ref, sc_ref, wq_ref, k1_ref, k2_ref, t_ref, s1_ref, s2_ref):
    t = (x_ref[...] * (1.0 + sc_ref[...]) + sh_ref[...]).astype(BF16)
    t_ref[...] = t
    q = _dot(t, wq_ref[...]).astype(BF16)
    half = PEER_NKEYS
    for h in range(PEER_HEADS):
        s1_ref[h] = _dot_nt(k1_ref[...], q[:, (2 * h) * half:(2 * h + 1) * half])
        s2_ref[h] = _dot_nt(k2_ref[...], q[:, (2 * h + 1) * half:(2 * h + 2) * half])


def peer_query(x2, sh, sc, wq, k1, k2, rows_per_group):
    n = x2.shape[0]
    tm = min(512, rows_per_group)
    grp = pl.BlockSpec((None, 1, D_MODEL), lambda i: ((i * tm) // rows_per_group, 0, 0))
    sco = pl.BlockSpec((PEER_HEADS, PEER_NKEYS, tm), lambda i: (0, 0, i))
    return pl.pallas_call(
        _peer_query_kernel,
        grid=(n // tm,),
        in_specs=[pl.BlockSpec((tm, D_MODEL), lambda i: (i, 0)), grp, grp,
                  pl.BlockSpec(wq.shape, lambda i: (0, 0)),
                  pl.BlockSpec(k1.shape, lambda i: (0, 0)),
                  pl.BlockSpec(k2.shape, lambda i: (0, 0))],
        out_specs=[pl.BlockSpec((tm, D_MODEL), lambda i: (i, 0)), sco, sco],
        out_shape=[jax.ShapeDtypeStruct((n, D_MODEL), BF16),
                   jax.ShapeDtypeStruct((PEER_HEADS, PEER_NKEYS, n), F32),
                   jax.ShapeDtypeStruct((PEER_HEADS, PEER_NKEYS, n), F32)],
        compiler_params=_params("parallel"),
        name="peer_query",
    )(x2, sh, sc, wq, k1, k2)


_CAND_GROUPS = ((0, 16), (1, 8), (2, 5), (3, 4), (4, 3), (5, 2), (6, 2), (7, 2))
_CAND_ROWS = 16 + 8 * 7 + 8


def _sorted_top(s, out_ref):
    cur = s
    for r in range(PEER_TOPK):
        m = jnp.max(cur, axis=0, keepdims=True)
        out_ref[r:r + 1, :] = m
        cur = jnp.where(cur >= m, -jnp.inf, cur)
    return jnp.max(cur, axis=0, keepdims=True)


def _peer_route_kernel(s1_ref, s2_ref, thr_ref, e1_ref, e2_ref, v1_ref, v2_ref, cand_ref):
    s1 = s1_ref[...]
    s2 = s2_ref[...]
    v1_next = _sorted_top(s1, v1_ref)
    v2_next = _sorted_top(s2, v2_ref)
    tn = s1.shape[1]
    row8 = lax.broadcasted_iota(jnp.int32, (SUBLANES, tn), 0)
    cand_ref[0:16, :] = v1_ref[0:1, :] + v2_ref[...]
    base = 16
    for i, cnt in _CAND_GROUPS[1:]:
        c = v1_ref[i:i + 1, :] + v2_ref[0:SUBLANES, :]
        cand_ref[base:base + SUBLANES, :] = jnp.where(row8 < cnt, c, -jnp.inf)
        base += SUBLANES
    cand_ref[base:base + SUBLANES, :] = v1_ref[SUBLANES:2 * SUBLANES, :] + v2_ref[0:1, :]
    cand = cand_ref[...]
    cur = cand
    tau = None
    for r in range(PEER_TOPK):
        tau = jnp.max(cur, axis=0, keepdims=True)
        cur = jnp.where(cur >= tau, -jnp.inf, cur)
    m1 = v1_ref[0:1, :]
    m2 = v2_ref[0:1, :]
    zsum = jnp.sum(jnp.where(cand >= tau, jnp.exp(cand - (m1 + m2)), 0.0), axis=0, keepdims=True)
    runner_up = jnp.maximum(jnp.max(cur, axis=0, keepdims=True),
                            jnp.maximum(m1 + v2_next, v1_next + m2))
    cut = 0.5 * (tau + runner_up)
    thr_ref[...] = cut - s1
    e1_ref[...] = jnp.exp(s1 - m1) / zsum
    e2_ref[...] = jnp.exp(s2 - m2)


def peer_route(s1t, s2t):
    _, _, n = s1t.shape
    tn = min(512, n)
    blk = pl.BlockSpec((None, PEER_NKEYS, tn), lambda h, i: (h, 0, i))
    shp = jax.ShapeDtypeStruct(s1t.shape, F32)
    return pl.pallas_call(
        _peer_route_kernel,
        grid=(PEER_HEADS, n // tn),
        in_specs=[blk, blk],
        out_specs=[blk, blk, blk],
        out_shape=[shp, shp, shp],
        scratch_shapes=[pltpu.VMEM((PEER_TOPK, tn), F32), pltpu.VMEM((PEER_TOPK, tn), F32),
                        pltpu.VMEM((_CAND_ROWS, tn), F32)],
        compiler_params=_params("parallel", "parallel"),
        name="peer_route",
    )(s1t, s2t)


def _gelu_tanh(x):
    return 0.5 * x * (1.0 + jnp.tanh(math.sqrt(2.0 / math.pi) * (x + 0.044715 * (x * x * x))))


def _peer_mix_kernel(t_ref, u_ref, vt_ref, s2_ref, e2_ref, thr_ref, e1_ref, x_ref, xg_ref,
                     lg_ref, lb_ref, o_ref, acc_ref, w_ref, *, tm, a_per_chunk, alpha):
    j = pl.program_id(1)

    @pl.when(j == 0)
    def _():
        acc_ref[...] = jnp.zeros_like(acc_ref)

    act = _dot_nt(u_ref[...], t_ref[...])
    for lt in range(tm // LANES):
        ls = slice(lt * LANES, (lt + 1) * LANES)
        for a in range(a_per_chunk):
            sel = jnp.zeros((PEER_NKEYS, LANES), F32)
            for h in range(PEER_HEADS):
                keep = s2_ref[h, :, ls] >= thr_ref[h, a:a + 1, ls]
                sel = sel + jnp.where(keep, e2_ref[h, :, ls], 0.0) * e1_ref[h, a:a + 1, ls]
            rs = slice(a * PEER_NKEYS, (a + 1) * PEER_NKEYS)
            w_ref[rs, ls] = (_gelu_tanh(act[rs, ls]) * sel).astype(BF16)
    acc_ref[...] += _dot(vt_ref[...], w_ref[...])

    @pl.when(j == pl.num_programs(1) - 1)
    def _():
        y = alpha * x_ref[...] + xg_ref[...] * acc_ref[...].T
        o_ref[...] = _layer_norm(y, lg_ref[...], lb_ref[...])


def peer_mix(t, u_bf16, vt_bf16, s2t, e2t, thrt, e1t, x2, xg, ln_g, ln_b, rows_per_group, alpha):
    n = x2.shape[0]
    tm = min(512, rows_per_group)
    a_per_chunk = 8
    ec = a_per_chunk * PEER_NKEYS
    n_chunks = u_bf16.shape[0] // ec
    full = pl.BlockSpec((PEER_HEADS, PEER_NKEYS, tm), lambda i, j: (0, 0, i))
    part = pl.BlockSpec((PEER_HEADS, a_per_chunk, tm), lambda i, j: (0, j, i))
    vec = pl.BlockSpec((1, D_MODEL), lambda i, j: (0, 0))
    grp = pl.BlockSpec((None, 1, D_MODEL), lambda i, j: ((i * tm) // rows_per_group, 0, 0))
    row = pl.BlockSpec((tm, D_MODEL), lambda i, j: (i, 0))
    return pl.pallas_call(
        functools.partial(_peer_mix_kernel, tm=tm, a_per_chunk=a_per_chunk, alpha=alpha),
        grid=(n // tm, n_chunks),
        in_specs=[row,
                  pl.BlockSpec((ec, D_MODEL), lambda i, j: (j, 0)),
                  pl.BlockSpec((D_MODEL, ec), lambda i, j: (0, j)),
                  full, full, part, part, row, grp, vec, vec],
        out_specs=row,
        out_shape=jax.ShapeDtypeStruct((n, D_MODEL), F32),
        scratch_shapes=[pltpu.VMEM((D_MODEL, tm), F32), pltpu.VMEM((ec, tm), BF16)],
        compiler_params=_params("parallel", "arbitrary"),
        name="peer_mix",
    )(t, u_bf16, vt_bf16, s2t, e2t, thrt, e1t, x2, xg, ln_g.reshape(1, -1), ln_b.reshape(1, -1))


def kernel(x, c, ctx, c_ctx, w_ada, b_ada, w_in, hg_lb_logits, hg_norm_g, w_hg_o, conv_dw, conv_b,
           conv_ln_g, conv_ln_b, w_conv_o, att_qn_g, att_kn_g, w_att_o, w_out, ln1_g, ln1_b,
           peer_wq, peer_k1, peer_k2, peer_u, peer_v, ln2_g, ln2_b):
    batch, seq, _ = x.shape
    ctx_len = ctx.shape[1]
    depth = w_ada.shape[0]
    alpha = (2 * depth) ** 0.25
    n_x = batch * seq
    n_c = batch * ctx_len

    lb_p = jax.nn.softmax(hg_lb_logits.astype(F32), axis=0)
    lb = jnp.cumsum(lb_p, axis=0) - lb_p
    cos, sin = _rope_tables(seq)
    cc = jnp.concatenate([c, c_ctx[None, :], jnp.zeros((SUBLANES - 1 - batch, D_MODEL), F32)], axis=0)

    x2 = x.reshape(n_x, D_MODEL)
    c2 = ctx.reshape(n_c, D_MODEL)
    zero_state = jnp.zeros((batch, HG_HEADS, HG_D, HG_D), F32)

    for l in range(depth):
        last = l == depth - 1
        mod = ada_mod(cc, w_ada[l], b_ada[l])
        mx = mod[:batch].reshape(batch, 1, N_ADA, D_MODEL)
        mc = mod[batch:batch + 1].reshape(1, 1, N_ADA, D_MODEL)
        xsh1, xsc1, xg1, xsh2, xsc2, xg2 = [mx[:, :, k] for k in range(N_ADA)]
        csh1, csc1, cg1, csh2, csc2, cg2 = [mc[:, :, k] for k in range(N_ADA)]

        w_in_b = w_in[l].astype(BF16)
        zx = in_proj(x2, xsh1, xsc1, w_in_b, seq)
        zc = in_proj(c2, csh1, csc1, w_in_b, n_c)

        lb_f = lb[l, 0].reshape(1, -1)
        lb_b = lb[l, 1].reshape(1, -1)
        ocf, ocb, st_f, st_b = hgrn_scan(zc, lb_f, lb_b, zero_state, zero_state, batch, ctx_len)
        oxf, oxb, _, _ = hgrn_scan(zx, lb_f, lb_b, st_f, st_b, batch, seq)

        qx, kx, vx = attn_qkv(zx, att_qn_g[l], att_kn_g[l], cos, sin, batch, seq, True)
        qc, kc, vc = attn_qkv(zc, att_qn_g[l], att_kn_g[l], cos, sin, batch, ctx_len, False)
        att_x = attention(qx, jnp.concatenate([kx, kc], axis=2), jnp.concatenate([vx, vc], axis=2))
        ucx = conformer_conv(zx, conv_dw[l], conv_b[l], conv_ln_g[l], conv_ln_b[l], batch, seq)

        w_hg, w_cv, w_at, w_o = (w.astype(BF16) for w in (w_hg_o[l], w_conv_o[l], w_att_o[l], w_out[l]))
        wq = peer_wq[l].astype(BF16)
        k1 = peer_k1[l].astype(BF16)
        k2 = peer_k2[l].astype(BF16)
        u_b = peer_u[l].astype(BF16)
        vt_b = peer_v[l].astype(BF16).T

        def channel_mix(xs, sh2, sc2, g2, rows_per_group):
            t, s1t, s2t = peer_query(xs, sh2, sc2, wq, k1, k2, rows_per_group)
            thr, e1, e2 = peer_route(s1t, s2t)
            return peer_mix(t, u_b, vt_b, s2t, e2, thr, e1, xs, g2, ln2_g[l], ln2_b[l],
                            rows_per_group, alpha)

        x2 = merge_ln(oxf, oxb, zx, ucx, att_x, x2, xg1, hg_norm_g[l], w_hg, w_cv, w_at, w_o,
                      ln1_g[l], ln1_b[l], seq, alpha)
        x2 = channel_mix(x2, xsh2, xsc2, xg2, seq)

        if not last:
            att_c = attention(qc, kc, vc)
            ucc = conformer_conv(zc, conv_dw[l], conv_b[l], conv_ln_g[l], conv_ln_b[l], batch, ctx_len)
            c2 = merge_ln(ocf, ocb, zc, ucc, att_c, c2, cg1, hg_norm_g[l], w_hg, w_cv, w_at, w_o,
                          ln1_g[l], ln1_b[l], n_c, alpha)
            c2 = channel_mix(c2, csh2, csc2, cg2, n_c)

    return x2.reshape(batch, seq, D_MODEL)
```

```python
import functools
import math

import numpy as np
import jax
import jax.numpy as jnp
from jax import lax
from jax.experimental import pallas as pl
from jax.experimental.pallas import tpu as pltpu

F32 = jnp.float32
BF16 = jnp.bfloat16

D_MODEL = 1024
N_ADA = 6
HG_HEADS = 8
HG_D = 128
CONV_K = 31
ATT_HEADS = 8
ATT_KV_HEADS = 4
ATT_GROUP = ATT_HEADS // ATT_KV_HEADS
HEAD_DIM = 128
GRID_W = 64
ROPE_THETA = 10000.0
PEER_HEADS = 8
PEER_NKEYS = 128
PEER_TOPK = 16
EPS = 1e-6
F32_TINY = float(np.finfo(np.float32).tiny)

LANES = 128
SUBLANES = 8
VMEM_LIMIT_BYTES = 52 * 1024 * 1024

COL_HG_Q, COL_HG_FF, COL_HG_FB, COL_HG_I, COL_HG_G = 0, 1, 2, 3, 4
COL_CONV_A, COL_CONV_G, COL_ATT_Q, COL_ATT_KV = 5, 6, 7, 8
COL_GATE0 = 9
IN_WIDTH = 12 * D_MODEL


def _params(*sem):
    return pltpu.CompilerParams(dimension_semantics=sem, vmem_limit_bytes=VMEM_LIMIT_BYTES)


def _dot(a, b):
    return jnp.dot(a, b, preferred_element_type=F32)


def _dot_nt(a, b):
    return lax.dot_general(a, b, (((1,), (1,)), ((), ())), preferred_element_type=F32)


def _dot_tn(a, b):
    return lax.dot_general(a, b, (((0,), (0,)), ((), ())), preferred_element_type=F32)


def _sigmoid(x):
    return 1.0 / (1.0 + jnp.exp(-x))


def _silu(x):
    return x * _sigmoid(x)


def _layer_norm(y, g, b):
    mu = jnp.mean(y, axis=-1, keepdims=True)
    d = y - mu
    var = jnp.mean(d * d, axis=-1, keepdims=True)
    return d * lax.rsqrt(var + EPS) * g + b


def _ada_kernel(c_ref, w_ref, b_ref, o_ref):
    s = _silu(c_ref[...]).astype(BF16)
    o_ref[...] = _dot(s, w_ref[...].astype(BF16)) + b_ref[...]


def ada_mod(cc, w, b):
    rows = cc.shape[0]
    tn = 1024
    return pl.pallas_call(
        _ada_kernel,
        grid=(w.shape[1] // tn,),
        in_specs=[
            pl.BlockSpec((rows, D_MODEL), lambda j: (0, 0)),
            pl.BlockSpec((D_MODEL, tn), lambda j: (0, j)),
            pl.BlockSpec((1, tn), lambda j: (0, j)),
        ],
        out_specs=pl.BlockSpec((rows, tn), lambda j: (0, j)),
        out_shape=jax.ShapeDtypeStruct((rows, w.shape[1]), F32),
        compiler_params=_params("parallel"),
        name="ada_mod",
    )(cc, w, b.reshape(1, -1))


def _inproj_kernel(x_ref, sh_ref, sc_ref, w_ref, o_ref, xm_ref):
    @pl.when(pl.program_id(1) == 0)
    def _():
        xm_ref[...] = (x_ref[...] * (1.0 + sc_ref[...]) + sh_ref[...]).astype(BF16)

    o_ref[...] = _dot(xm_ref[...], w_ref[...])


def in_proj(x2, sh, sc, w_bf16, rows_per_group):
    n = x2.shape[0]
    tm = min(1024, rows_per_group)
    tn = 1024
    grp = lambda i, j: ((i * tm) // rows_per_group, 0, 0)
    return pl.pallas_call(
        _inproj_kernel,
        grid=(n // tm, IN_WIDTH // tn),
        in_specs=[
            pl.BlockSpec((tm, D_MODEL), lambda i, j: (i, 0)),
            pl.BlockSpec((None, 1, D_MODEL), grp),
            pl.BlockSpec((None, 1, D_MODEL), grp),
            pl.BlockSpec((D_MODEL, tn), lambda i, j: (0, j)),
        ],
        out_specs=pl.BlockSpec((tm, tn), lambda i, j: (i, j)),
        out_shape=jax.ShapeDtypeStruct((n, IN_WIDTH), F32),
        scratch_shapes=[pltpu.VMEM((tm, D_MODEL), BF16)],
        compiler_params=_params("parallel", "arbitrary"),
        name="in_proj",
    )(x2, sh, sc, w_bf16)


def _hgrn_constants(chunk, reverse):
    c = chunk
    levels = []
    m = c // 2
    while m >= 1:
        levels.append(m)
        m //= 2
    t = np.arange(c)[:, None]
    u = np.arange(c)[None, :]
    mats = [(u <= t), (u > t)]
    masks = [np.eye(c, dtype=bool)]
    for m in levels:
        blk = t // (2 * m)
        r = blk * 2 * m + m
        upper = (t % (2 * m)) >= m
        a = np.where(upper, (u > r) & (u <= t), (u > t) & (u <= r))
        mats.append(a)
        ts = np.arange(c)[:, None]
        ss = np.arange(c)[None, :]
        same = (ts // (2 * m)) == (ss // (2 * m))
        masks.append(same & ((ts % (2 * m)) >= m) & ((ss % (2 * m)) < m))
    amat = np.concatenate([np.asarray(a, np.float32) for a in mats], axis=0)
    msk = np.stack([np.asarray(a, np.float32) for a in masks], axis=0)
    if reverse:
        amat = amat.reshape(len(mats), c, c)[:, ::-1, ::-1].reshape(len(mats) * c, c)
        msk = msk[:, ::-1, ::-1]
    return jnp.asarray(amat, BF16), jnp.asarray(np.ascontiguousarray(msk), F32), len(levels)


def _hgrn_direction(q, z, v, lb, amat, mask_ref, st, chunk, n_levels, last_row):
    c = chunk
    f = lb + (1.0 - lb) * _sigmoid(z)
    lf = jnp.log(jnp.maximum(f, F32_TINY))
    kk = (1.0 - lb) * _sigmoid(-z)
    hi = lf.astype(BF16)
    lo = (lf - hi.astype(F32)).astype(BF16)
    g = _dot(amat, jnp.concatenate([hi, lo], axis=1))
    dall = g[:, :HG_D] + g[:, HG_D:]
    bq = dall[0:c]
    bk = dall[c:2 * c]
    vb = v.astype(BF16)
    scores = _dot_nt(q.astype(BF16), kk.astype(BF16)) * mask_ref[0]
    for m in range(n_levels):
        e = jnp.exp(dall[(2 + m) * c:(3 + m) * c])
        scores = scores + _dot_nt((q * e).astype(BF16), (kk * e).astype(BF16)) * mask_ref[m + 1]
    qd = (q * jnp.exp(bq)).astype(BF16)
    kd = (kk * jnp.exp(bk)).astype(BF16)
    o = _dot(scores.astype(BF16), vb) + _dot_nt(qd, st.astype(BF16))
    st_new = st * jnp.exp(bq[last_row:last_row + 1]) + _dot_tn(vb, kd)
    return o, st_new


def _hgrn_kernel(qf_ref, zf_ref, vf_ref, qb_ref, zb_ref, vb_ref, lbf_ref, lbb_ref,
                 af_ref, mf_ref, ab_ref, mb_ref, s0f_ref, s0b_ref,
                 of_ref, ob_ref, sf_ref, sb_ref, *, chunk, n_levels, heads):
    @pl.when(pl.program_id(2) == 0)
    def _():
        sf_ref[...] = s0f_ref[...]
        sb_ref[...] = s0b_ref[...]

    af = af_ref[...]
    ab = ab_ref[...]
    for h in range(heads):
        sl = slice(h * HG_D, (h + 1) * HG_D)
        o, st = _hgrn_direction(_silu(qf_ref[:, sl]), zf_ref[:, sl], vf_ref[:, sl], lbf_ref[:, sl],
                                af, mf_ref, sf_ref[h], chunk, n_levels, chunk - 1)
        of_ref[:, sl] = o
        sf_ref[h] = st
        o, st = _hgrn_direction(_silu(qb_ref[:, sl]), zb_ref[:, sl], vb_ref[:, sl], lbb_ref[:, sl],
                                ab, mb_ref, sb_ref[h], chunk, n_levels, 0)
        ob_ref[:, sl] = o
        sb_ref[h] = st


def hgrn_scan(z, lb_f, lb_b, s0f, s0b, batch, seq, chunk=128, heads=2):
    n = batch * seq
    nb = seq // chunk
    hw = heads * HG_D
    cpb = D_MODEL // hw
    af, mf, n_levels = _hgrn_constants(chunk, False)
    ab, mb, _ = _hgrn_constants(chunk, True)
    fwd = lambda seg: pl.BlockSpec((chunk, hw), lambda b, h, c: (b * nb + c, seg * cpb + h))
    bwd = lambda seg: pl.BlockSpec((chunk, hw), lambda b, h, c: (b * nb + nb - 1 - c, seg * cpb + h))
    const2 = lambda a: pl.BlockSpec(a.shape, lambda b, h, c: (0, 0))
    const3 = lambda a: pl.BlockSpec(a.shape, lambda b, h, c: (0, 0, 0))
    state = pl.BlockSpec((None, heads, HG_D, HG_D), lambda b, h, c: (b, h, 0, 0))
    lbspec = pl.BlockSpec((1, hw), lambda b, h, c: (0, h))
    kern = functools.partial(_hgrn_kernel, chunk=chunk, n_levels=n_levels, heads=heads)
    return pl.pallas_call(
        kern,
        grid=(batch, HG_HEADS // heads, nb),
        in_specs=[fwd(COL_HG_Q), fwd(COL_HG_FF), fwd(COL_HG_I),
                  bwd(COL_HG_Q), bwd(COL_HG_FB), bwd(COL_HG_I),
                  lbspec, lbspec, const2(af), const3(mf), const2(ab), const3(mb), state, state],
        out_specs=[pl.BlockSpec((chunk, hw), lambda b, h, c: (b * nb + c, h)),
                   pl.BlockSpec((chunk, hw), lambda b, h, c: (b * nb + nb - 1 - c, h)),
                   state, state],
        out_shape=[jax.ShapeDtypeStruct((n, D_MODEL), F32), jax.ShapeDtypeStruct((n, D_MODEL), F32),
                   jax.ShapeDtypeStruct(s0f.shape, F32), jax.ShapeDtypeStruct(s0b.shape, F32)],
        compiler_params=_params("parallel", "parallel", "arbitrary"),
        name="hgrn_scan",
    )(z, z, z, z, z, z, lb_f, lb_b, af, mf, ab, mb, s0f, s0b)


CONV_HALO = 16


def _conv_kernel(a_ref, g_ref, ap_ref, gp_ref, an_ref, gn_ref, dw_ref, db_ref, lg_ref, lb_ref,
                 o_ref, u_ref, *, tl):
    i = pl.program_id(1)
    nblk = pl.num_programs(1)
    glu = lambda a, g: a * _sigmoid(g)
    prev_ok = (i > 0).astype(F32)
    next_ok = (i < nblk - 1).astype(F32)
    u_ref[0:CONV_HALO, :] = glu(ap_ref[...], gp_ref[...]) * prev_ok
    u_ref[CONV_HALO:CONV_HALO + tl, :] = glu(a_ref[...], g_ref[...])
    u_ref[CONV_HALO + tl:, :] = glu(an_ref[...], gn_ref[...]) * next_ok
    off = CONV_HALO - CONV_K // 2
    acc = jnp.zeros((tl, D_MODEL), F32) + db_ref[...]
    for k in range(CONV_K):
        acc = acc + u_ref[off + k:off + k + tl, :] * dw_ref[k:k + 1, :]
    y = _layer_norm(acc, lg_ref[...], lb_ref[...])
    o_ref[...] = _silu(y).astype(BF16)


def conformer_conv(z, dw, db, ln_g, ln_b, batch, seq):
    n = batch * seq
    tl = min(256, seq)
    nblk = seq // tl
    hb = tl // CONV_HALO
    hpb = seq // CONV_HALO
    cur = lambda col: pl.BlockSpec((tl, D_MODEL), lambda b, i: (b * nblk + i, col))
    prev = lambda col: pl.BlockSpec(
        (CONV_HALO, D_MODEL), lambda b, i: (b * hpb + jnp.maximum(i * hb - 1, 0), col))
    nxt = lambda col: pl.BlockSpec(
        (CONV_HALO, D_MODEL), lambda b, i: (b * hpb + jnp.minimum((i + 1) * hb, hpb - 1), col))
    vec = lambda r: pl.BlockSpec((r, D_MODEL), lambda b, i: (0, 0))
    return pl.pallas_call(
        functools.partial(_conv_kernel, tl=tl),
        grid=(batch, nblk),
        in_specs=[cur(COL_CONV_A), cur(COL_CONV_G), prev(COL_CONV_A), prev(COL_CONV_G),
                  nxt(COL_CONV_A), nxt(COL_CONV_G), vec(CONV_K), vec(1), vec(1), vec(1)],
        out_specs=pl.BlockSpec((tl, D_MODEL), lambda b, i: (b * nblk + i, 0)),
        out_shape=jax.ShapeDtypeStruct((n, D_MODEL), BF16),
        scratch_shapes=[pltpu.VMEM((tl + 2 * CONV_HALO, D_MODEL), F32)],
        compiler_params=_params("parallel", "parallel"),
        name="conformer_conv",
    )(z, z, z, z, z, z, dw, db.reshape(1, -1), ln_g.reshape(1, -1), ln_b.reshape(1, -1))


def _rope_tables(seq):
    n_rows = seq // GRID_W
    row = jnp.repeat(jnp.arange(n_rows), GRID_W).astype(F32)
    col = jnp.tile(jnp.arange(GRID_W), n_rows).astype(F32)
    n_freq = HEAD_DIM // 4
    inv = ROPE_THETA ** (-jnp.arange(n_freq, dtype=F32) / n_freq)
    ang = jnp.concatenate([row[:, None] * inv, col[:, None] * inv], axis=-1)
    cos = jnp.repeat(jnp.cos(ang), 2, axis=-1)
    sin = jnp.repeat(jnp.sin(ang), 2, axis=-1)
    sign = jnp.tile(jnp.asarray([-1.0, 1.0], F32), HEAD_DIM // 2)
    return cos, sin * sign


def _qkv_kernel(q_ref, kv_ref, qg_ref, kg_ref, cos_ref, sin_ref, qo_ref, ko_ref, vo_ref, *, rope):
    lane = lax.broadcasted_iota(jnp.int32, (1, HEAD_DIM), 1)
    even = (lane % 2) == 0

    def norm_rot(x, g):
        y = x * lax.rsqrt(jnp.mean(x * x, axis=-1, keepdims=True) + EPS) * g
        if rope:
            partner = jnp.where(even, pltpu.roll(y, HEAD_DIM - 1, 1), pltpu.roll(y, 1, 1))
            y = y * cos_ref[...] + partner * sin_ref[...]
        return y

    for h in range(ATT_HEADS):
        x = q_ref[:, h * HEAD_DIM:(h + 1) * HEAD_DIM]
        qo_ref[h] = (norm_rot(x, qg_ref[...]) * (HEAD_DIM ** -0.5)).astype(BF16)
    for h in range(ATT_KV_HEADS):
        x = kv_ref[:, h * HEAD_DIM:(h + 1) * HEAD_DIM]
        ko_ref[h] = norm_rot(x, kg_ref[...]).astype(BF16)
        off = (ATT_KV_HEADS + h) * HEAD_DIM
        vo_ref[h] = kv_ref[:, off:off + HEAD_DIM].astype(BF16)


def attn_qkv(z, qn_g, kn_g, cos, sin, batch, seq, rope):
    tl = min(512, seq)
    nblk = seq // tl
    tab = pl.BlockSpec((tl, HEAD_DIM), lambda b, i: (i, 0))
    vec = pl.BlockSpec((1, HEAD_DIM), lambda b, i: (0, 0))
    out = lambda nh: pl.BlockSpec((None, nh, tl, HEAD_DIM), lambda b, i: (b, 0, i, 0))
    return pl.pallas_call(
        functools.partial(_qkv_kernel, rope=rope),
        grid=(batch, nblk),
        in_specs=[pl.BlockSpec((tl, D_MODEL), lambda b, i: (b * nblk + i, COL_ATT_Q)),
                  pl.BlockSpec((tl, D_MODEL), lambda b, i: (b * nblk + i, COL_ATT_KV)),
                  vec, vec, tab, tab],
        out_specs=[out(ATT_HEADS), out(ATT_KV_HEADS), out(ATT_KV_HEADS)],
        out_shape=[jax.ShapeDtypeStruct((batch, ATT_HEADS, seq, HEAD_DIM), BF16),
                   jax.ShapeDtypeStruct((batch, ATT_KV_HEADS, seq, HEAD_DIM), BF16),
                   jax.ShapeDtypeStruct((batch, ATT_KV_HEADS, seq, HEAD_DIM), BF16)],
        compiler_params=_params("parallel", "parallel"),
        name="attn_qkv",
    )(z, z, qn_g.reshape(1, -1), kn_g.reshape(1, -1), cos, sin)


def _attn_kernel(q_ref, k_ref, v_ref, o_ref, *, tq):
    q = q_ref[...].reshape(ATT_GROUP * tq, HEAD_DIM)
    s = _dot_nt(q, k_ref[...])
    m = jnp.max(s, axis=-1, keepdims=True)
    p = jnp.exp(s - m)
    l = jnp.sum(p, axis=-1, keepdims=True)
    o = _dot(p.astype(BF16), v_ref[...]) / l
    for g in range(ATT_GROUP):
        o_ref[:, g * HEAD_DIM:(g + 1) * HEAD_DIM] = o[g * tq:(g + 1) * tq].astype(BF16)


def attention(q, k, v):
    batch, _, lq, _ = q.shape
    s_len = k.shape[2]
    tq = min(128, lq)
    nq = lq // tq
    kv = pl.BlockSpec((None, None, s_len, HEAD_DIM), lambda b, kh, i: (b, kh, 0, 0))
    return pl.pallas_call(
        functools.partial(_attn_kernel, tq=tq),
        grid=(batch, ATT_KV_HEADS, nq),
        in_specs=[pl.BlockSpec((None, ATT_GROUP, tq, HEAD_DIM), lambda b, kh, i: (b, kh, i, 0)), kv, kv],
        out_specs=pl.BlockSpec((tq, ATT_GROUP * HEAD_DIM), lambda b, kh, i: (b * nq + i, kh)),
        out_shape=jax.ShapeDtypeStruct((batch * lq, D_MODEL), BF16),
        compiler_params=_params("parallel", "parallel", "arbitrary"),
        name="attention",
    )(q, k, v)


def _merge_kernel(of_ref, ob_ref, hg_ref, uc_ref, at_ref, g0_ref, g1_ref, g2_ref, x_ref, xg_ref,
                  ng_ref, whg_ref, wcv_ref, wat_ref, wout_ref, lg_ref, lb_ref, o_ref, *, alpha):
    o = of_ref[...] + ob_ref[...]
    parts = []
    for h in range(HG_HEADS):
        oh = o[:, h * HG_D:(h + 1) * HG_D]
        parts.append(oh * lax.rsqrt(jnp.mean(oh * oh, axis=-1, keepdims=True) + EPS) * ng_ref[...])
    on = jnp.concatenate(parts, axis=1) * _silu(hg_ref[...])
    b_hg = _dot(on.astype(BF16), whg_ref[...])
    b_cv = _dot(uc_ref[...], wcv_ref[...])
    b_at = _dot(at_ref[...], wat_ref[...])
    mix_in = _sigmoid(g0_ref[...]) * b_hg + _sigmoid(g1_ref[...]) * b_cv + _sigmoid(g2_ref[...]) * b_at
    mix = _dot(mix_in.astype(BF16), wout_ref[...])
    y = alpha * x_ref[...] + xg_ref[...] * mix
    o_ref[...] = _layer_norm(y, lg_ref[...], lb_ref[...])


def merge_ln(o_f, o_b, z, uc, att, x2, xg, norm_g, w_hg, w_cv, w_at, w_out, ln_g, ln_b,
             rows_per_group, alpha):
    n = x2.shape[0]
    tm = min(256, rows_per_group)
    row = lambda col: pl.BlockSpec((tm, D_MODEL), lambda i: (i, col))
    wspec = pl.BlockSpec((D_MODEL, D_MODEL), lambda i: (0, 0))
    vec = lambda w: pl.BlockSpec((1, w), lambda i: (0, 0))
    grp = pl.BlockSpec((None, 1, D_MODEL), lambda i: ((i * tm) // rows_per_group, 0, 0))
    return pl.pallas_call(
        functools.partial(_merge_kernel, alpha=alpha),
        grid=(n // tm,),
        in_specs=[row(0), row(0), row(COL_HG_G), row(0), row(0),
                  row(COL_GATE0), row(COL_GATE0 + 1), row(COL_GATE0 + 2), row(0), grp,
                  vec(HG_D), wspec, wspec, wspec, wspec, vec(D_MODEL), vec(D_MODEL)],
        out_specs=row(0),
        out_shape=jax.ShapeDtypeStruct((n, D_MODEL), F32),
        compiler_params=_params("parallel"),
        name="merge_ln",
    )(o_f, o_b, z, uc, att, z, z, z, x2, xg, norm_g.reshape(1, -1),
      w_hg, w_cv, w_at, w_out, ln_g.reshape(1, -1), ln_b.reshape(1, -1))


def _peer_query_kernel(x_ref, sh_ref, sc_ref, wq_ref, k1_ref, k2_ref, t_ref, s1_ref, s2_ref):
    t = (x_ref[...] * (1.0 + sc_ref[...]) + sh_ref[...]).astype(BF16)
    t_ref[...] = t
    q = _dot(t, wq_ref[...]).astype(BF16)
    half = PEER_NKEYS
    for h in range(PEER_HEADS):
        s1_ref[h] = _dot_nt(k1_ref[...], q[:, (2 * h) * half:(2 * h + 1) * half])
        s2_ref[h] = _dot_nt(k2_ref[...], q[:, (2 * h + 1) * half:(2 * h + 2) * half])


def peer_query(x2, sh, sc, wq, k1, k2, rows_per_group):
    n = x2.shape[0]
    tm = min(512, rows_per_group)
    grp = pl.BlockSpec((None, 1, D_MODEL), lambda i: ((i * tm) // rows_per_group, 0, 0))
    sco = pl.BlockSpec((PEER_HEADS, PEER_NKEYS, tm), lambda i: (0, 0, i))
    return pl.pallas_call(
        _peer_query_kernel,
        grid=(n // tm,),
        in_specs=[pl.BlockSpec((tm, D_MODEL), lambda i: (i, 0)), grp, grp,
                  pl.BlockSpec(wq.shape, lambda i: (0, 0)),
                  pl.BlockSpec(k1.shape, lambda i: (0, 0)),
                  pl.BlockSpec(k2.shape, lambda i: (0, 0))],
        out_specs=[pl.BlockSpec((tm, D_MODEL), lambda i: (i, 0)), sco, sco],
        out_shape=[jax.ShapeDtypeStruct((n, D_MODEL), BF16),
                   jax.ShapeDtypeStruct((PEER_HEADS, PEER_NKEYS, n), F32),
                   jax.ShapeDtypeStruct((PEER_HEADS, PEER_NKEYS, n), F32)],
        compiler_params=_params("parallel"),
        name="peer_query",
    )(x2, sh, sc, wq, k1, k2)


_CAND_GROUPS = ((0, 16), (1, 8), (2, 5), (3, 4), (4, 3), (5, 2), (6, 2), (7, 2))
_CAND_ROWS = 16 + 8 * 7 + 8


def _sorted_top(s, out_ref):
    cur = s
    for r in range(PEER_TOPK):
        m = jnp.max(cur, axis=0, keepdims=True)
        out_ref[r:r + 1, :] = m
        cur = jnp.where(cur >= m, -jnp.inf, cur)
    return jnp.max(cur, axis=0, keepdims=True)


def _peer_route_kernel(s1_ref, s2_ref, thr_ref, e1_ref, e2_ref, v1_ref, v2_ref, cand_ref):
    s1 = s1_ref[...]
    s2 = s2_ref[...]
    v1_next = _sorted_top(s1, v1_ref)
    v2_next = _sorted_top(s2, v2_ref)
    tn = s1.shape[1]
    row8 = lax.broadcasted_iota(jnp.int32, (SUBLANES, tn), 0)
    cand_ref[0:16, :] = v1_ref[0:1, :] + v2_ref[...]
    base = 16
    for i, cnt in _CAND_GROUPS[1:]:
        c = v1_ref[i:i + 1, :] + v2_ref[0:SUBLANES, :]
        cand_ref[base:base + SUBLANES, :] = jnp.where(row8 < cnt, c, -jnp.inf)
        base += SUBLANES
    cand_ref[base:base + SUBLANES, :] = v1_ref[SUBLANES:2 * SUBLANES, :] + v2_ref[0:1, :]
    cand = cand_ref[...]
    cur = cand
    tau = None
    for r in range(PEER_TOPK):
        tau = jnp.max(cur, axis=0, keepdims=True)
        cur = jnp.where(cur >= tau, -jnp.inf, cur)
    m1 = v1_ref[0:1, :]
    m2 = v2_ref[0:1, :]
    zsum = jnp.sum(jnp.where(cand >= tau, jnp.exp(cand - (m1 + m2)), 0.0), axis=0, keepdims=True)
    runner_up = jnp.maximum(jnp.max(cur, axis=0, keepdims=True),
                            jnp.maximum(m1 + v2_next, v1_next + m2))
    cut = 0.5 * (tau + runner_up)
    thr_ref[...] = jnp.exp(cut - s1 - m2)
    e1_ref[...] = jnp.exp(s1 - m1) * (0.5 / zsum)
    e2 = jnp.exp(s2 - m2)
    for p in range(tn // PEER_TOKEN_SUB):
        e2_ref[p] = e2[:, p * PEER_TOKEN_SUB:(p + 1) * PEER_TOKEN_SUB]


PEER_TOKEN_SUB = 256


def peer_route(s1t, s2t):
    _, _, n = s1t.shape
    tn = min(512, n)
    blk = pl.BlockSpec((None, PEER_NKEYS, tn), lambda h, i: (h, 0, i))
    sub = pl.BlockSpec((None, tn // PEER_TOKEN_SUB, PEER_NKEYS, PEER_TOKEN_SUB), lambda h, i: (h, i, 0, 0))
    shp = jax.ShapeDtypeStruct(s1t.shape, F32)
    return pl.pallas_call(
        _peer_route_kernel,
        grid=(PEER_HEADS, n // tn),
        in_specs=[blk, blk],
        out_specs=[blk, blk, sub],
        out_shape=[shp, shp, jax.ShapeDtypeStruct(
            (PEER_HEADS, n // PEER_TOKEN_SUB, PEER_NKEYS, PEER_TOKEN_SUB), F32)],
        scratch_shapes=[pltpu.VMEM((PEER_TOPK, tn), F32), pltpu.VMEM((PEER_TOPK, tn), F32),
                        pltpu.VMEM((_CAND_ROWS, tn), F32)],
        compiler_params=_params("parallel", "parallel"),
        name="peer_route",
    )(s1t, s2t)


def _peer_mix_kernel(t_ref, u_ref, vt_ref, e2_ref, thr_ref, e1_ref, x_ref, xg_ref, lg_ref, lb_ref,
                     o_ref, acc_ref, act_ref, w_ref, *, n_sub, a_per_chunk, alpha):
    j = pl.program_id(1)

    @pl.when(j == 0)
    def _():
        acc_ref[...] = jnp.zeros_like(acc_ref)

    for p in range(n_sub):
        rows = slice(p * PEER_TOKEN_SUB, (p + 1) * PEER_TOKEN_SUB)
        act_ref[p] = _dot_nt(u_ref[...], t_ref[rows, :])

    c0 = math.sqrt(2.0 / math.pi)
    c1 = c0 * 0.044715

    def sub_block(p, carry):
        for lt in range(PEER_TOKEN_SUB // LANES):
            ls = slice(lt * LANES, (lt + 1) * LANES)
            for a in range(a_per_chunk):
                sel = jnp.zeros((PEER_NKEYS, LANES), F32)
                for h in range(PEER_HEADS):
                    e2 = e2_ref[h, p, :, ls]
                    keep = e2 >= thr_ref[h, a, p, :, ls]
                    sel = sel + jnp.where(keep, e2, 0.0) * e1_ref[h, a, p, :, ls]
                rs = slice(a * PEER_NKEYS, (a + 1) * PEER_NKEYS)
                xa = act_ref[p, rs, ls]
                th = jnp.tanh(xa * (c0 + c1 * (xa * xa)))
                w_ref[p, rs, ls] = ((xa + xa * th) * sel).astype(BF16)
        return carry

    lax.fori_loop(0, n_sub, sub_block, 0)

    for p in range(n_sub):
        acc_ref[p] += _dot(vt_ref[...], w_ref[p])

    @pl.when(j == pl.num_programs(1) - 1)
    def _():
        for p in range(n_sub):
            rows = slice(p * PEER_TOKEN_SUB, (p + 1) * PEER_TOKEN_SUB)
            y = alpha * x_ref[rows, :] + xg_ref[...] * acc_ref[p].T
            o_ref[rows, :] = _layer_norm(y, lg_ref[...], lb_ref[...])


def peer_mix(t, u_bf16, vt_bf16, e2s, thrt, e1t, x2, xg, ln_g, ln_b, rows_per_group, alpha):
    n = x2.shape[0]
    tm = min(512, rows_per_group)
    n_sub = tm // PEER_TOKEN_SUB
    a_per_chunk = 8
    ec = a_per_chunk * PEER_NKEYS
    n_chunks = u_bf16.shape[0] // ec
    per_a = (PEER_HEADS, PEER_NKEYS, n // PEER_TOKEN_SUB, 1, PEER_TOKEN_SUB)
    part = pl.BlockSpec((PEER_HEADS, a_per_chunk, n_sub, 1, PEER_TOKEN_SUB),
                        lambda i, j: (0, j, i, 0, 0))
    vec = pl.BlockSpec((1, D_MODEL), lambda i, j: (0, 0))
    grp = pl.BlockSpec((None, 1, D_MODEL), lambda i, j: ((i * tm) // rows_per_group, 0, 0))
    row = pl.BlockSpec((tm, D_MODEL), lambda i, j: (i, 0))
    buf = lambda dt: pltpu.VMEM((n_sub, ec, PEER_TOKEN_SUB), dt)
    return pl.pallas_call(
        functools.partial(_peer_mix_kernel, n_sub=n_sub, a_per_chunk=a_per_chunk, alpha=alpha),
        grid=(n // tm, n_chunks),
        in_specs=[row,
                  pl.BlockSpec((ec, D_MODEL), lambda i, j: (j, 0)),
                  pl.BlockSpec((D_MODEL, ec), lambda i, j: (0, j)),
                  pl.BlockSpec((PEER_HEADS, n_sub, PEER_NKEYS, PEER_TOKEN_SUB), lambda i, j: (0, i, 0, 0)),
                  part, part, row, grp, vec, vec],
        out_specs=row,
        out_shape=jax.ShapeDtypeStruct((n, D_MODEL), F32),
        scratch_shapes=[pltpu.VMEM((n_sub, D_MODEL, PEER_TOKEN_SUB), F32), buf(F32), buf(BF16)],
        compiler_params=_params("parallel", "arbitrary"),
        name="peer_mix",
    )(t, u_bf16, vt_bf16, e2s, thrt.reshape(per_a), e1t.reshape(per_a), x2, xg,
      ln_g.reshape(1, -1), ln_b.reshape(1, -1))


def kernel(x, c, ctx, c_ctx, w_ada, b_ada, w_in, hg_lb_logits, hg_norm_g, w_hg_o, conv_dw, conv_b,
           conv_ln_g, conv_ln_b, w_conv_o, att_qn_g, att_kn_g, w_att_o, w_out, ln1_g, ln1_b,
           peer_wq, peer_k1, peer_k2, peer_u, peer_v, ln2_g, ln2_b):
    batch, seq, _ = x.shape
    ctx_len = ctx.shape[1]
    depth = w_ada.shape[0]
    alpha = (2 * depth) ** 0.25
    n_x = batch * seq
    n_c = batch * ctx_len

    lb_p = jax.nn.softmax(hg_lb_logits.astype(F32), axis=0)
    lb = jnp.cumsum(lb_p, axis=0) - lb_p
    cos, sin = _rope_tables(seq)
    cc = jnp.concatenate([c, c_ctx[None, :], jnp.zeros((SUBLANES - 1 - batch, D_MODEL), F32)], axis=0)

    x2 = x.reshape(n_x, D_MODEL)
    c2 = ctx.reshape(n_c, D_MODEL)
    zero_state = jnp.zeros((batch, HG_HEADS, HG_D, HG_D), F32)

    for l in range(depth):
        last = l == depth - 1
        mod = ada_mod(cc, w_ada[l], b_ada[l])
        mx = mod[:batch].reshape(batch, 1, N_ADA, D_MODEL)
        mc = mod[batch:batch + 1].reshape(1, 1, N_ADA, D_MODEL)
        xsh1, xsc1, xg1, xsh2, xsc2, xg2 = [mx[:, :, k] for k in range(N_ADA)]
        csh1, csc1, cg1, csh2, csc2, cg2 = [mc[:, :, k] for k in range(N_ADA)]

        w_in_b = w_in[l].astype(BF16)
        zx = in_proj(x2, xsh1, xsc1, w_in_b, seq)
        zc = in_proj(c2, csh1, csc1, w_in_b, n_c)

        lb_f = lb[l, 0].reshape(1, -1)
        lb_b = lb[l, 1].reshape(1, -1)
        ocf, ocb, st_f, st_b = hgrn_scan(zc, lb_f, lb_b, zero_state, zero_state, batch, ctx_len)
        oxf, oxb, _, _ = hgrn_scan(zx, lb_f, lb_b, st_f, st_b, batch, seq)

        qx, kx, vx = attn_qkv(zx, att_qn_g[l], att_kn_g[l], cos, sin, batch, seq, True)
        qc, kc, vc = attn_qkv(zc, att_qn_g[l], att_kn_g[l], cos, sin, batch, ctx_len, False)
        att_x = attention(qx, jnp.concatenate([kx, kc], axis=2), jnp.concatenate([vx, vc], axis=2))
        ucx = conformer_conv(zx, conv_dw[l], conv_b[l], conv_ln_g[l], conv_ln_b[l], batch, seq)

        w_hg, w_cv, w_at, w_o = (w.astype(BF16) for w in (w_hg_o[l], w_conv_o[l], w_att_o[l], w_out[l]))
        wq = peer_wq[l].astype(BF16)
        k1 = peer_k1[l].astype(BF16)
        k2 = peer_k2[l].astype(BF16)
        u_b = peer_u[l].astype(BF16)
        vt_b = peer_v[l].astype(BF16).T

        def channel_mix(xs, sh2, sc2, g2, rows_per_group):
            t, s1t, s2t = peer_query(xs, sh2, sc2, wq, k1, k2, rows_per_group)
            thr, e1, e2 = peer_route(s1t, s2t)
            return peer_mix(t, u_b, vt_b, e2, thr, e1, xs, g2, ln2_g[l], ln2_b[l],
                            rows_per_group, alpha)

        x2 = merge_ln(oxf, oxb, zx, ucx, att_x, x2, xg1, hg_norm_g[l], w_hg, w_cv, w_at, w_o,
                      ln1_g[l], ln1_b[l], seq, alpha)
        x2 = channel_mix(x2, xsh2, xsc2, xg2, seq)

        if not last:
            att_c = attention(qc, kc, vc)
            ucc = conformer_conv(zc, conv_dw[l], conv_b[l], conv_ln_g[l], conv_ln_b[l], batch, ctx_len)
            c2 = merge_ln(ocf, ocb, zc, ucc, att_c, c2, cg1, hg_norm_g[l], w_hg, w_cv, w_at, w_o,
                          ln1_g[l], ln1_b[l], n_c, alpha)
            c2 = channel_mix(c2, csh2, csc2, cg2, n_c)

    return x2.reshape(batch, seq, D_MODEL)
```

```python
import functools
import math

import numpy as np
import jax
import jax.numpy as jnp
from jax import lax
from jax.experimental import pallas as pl
from jax.experimental.pallas import tpu as pltpu

F32 = jnp.float32
BF16 = jnp.bfloat16

D_MODEL = 1024
N_ADA = 6
HG_HEADS = 8
HG_D = 128
CONV_K = 31
ATT_HEADS = 8
ATT_KV_HEADS = 4
ATT_GROUP = ATT_HEADS // ATT_KV_HEADS
HEAD_DIM = 128
GRID_W = 64
ROPE_THETA = 10000.0
PEER_HEADS = 8
PEER_NKEYS = 128
PEER_TOPK = 16
EPS = 1e-6
F32_TINY = float(np.finfo(np.float32).tiny)

LANES = 128
SUBLANES = 8
VMEM_LIMIT_BYTES = 52 * 1024 * 1024

COL_HG_Q, COL_HG_FF, COL_HG_FB, COL_HG_I, COL_HG_G = 0, 1, 2, 3, 4
COL_CONV_A, COL_CONV_G, COL_ATT_Q, COL_ATT_KV = 5, 6, 7, 8
COL_GATE0 = 9
IN_WIDTH = 12 * D_MODEL


def _params(*sem):
    return pltpu.CompilerParams(dimension_semantics=sem, vmem_limit_bytes=VMEM_LIMIT_BYTES)


def _dot(a, b):
    return jnp.dot(a, b, preferred_element_type=F32)


def _dot_nt(a, b):
    return lax.dot_general(a, b, (((1,), (1,)), ((), ())), preferred_element_type=F32)


def _dot_tn(a, b):
    return lax.dot_general(a, b, (((0,), (0,)), ((), ())), preferred_element_type=F32)


def _sigmoid(x):
    return 1.0 / (1.0 + jnp.exp(-x))


def _silu(x):
    return x * _sigmoid(x)


def _layer_norm(y, g, b):
    mu = jnp.mean(y, axis=-1, keepdims=True)
    d = y - mu
    var = jnp.mean(d * d, axis=-1, keepdims=True)
    return d * lax.rsqrt(var + EPS) * g + b


def _ada_kernel(c_ref, w_ref, b_ref, o_ref):
    s = _silu(c_ref[...]).astype(BF16)
    o_ref[...] = _dot(s, w_ref[...].astype(BF16)) + b_ref[...]


def ada_mod(cc, w, b):
    rows = cc.shape[0]
    tn = 1024
    return pl.pallas_call(
        _ada_kernel,
        grid=(w.shape[1] // tn,),
        in_specs=[
            pl.BlockSpec((rows, D_MODEL), lambda j: (0, 0)),
            pl.BlockSpec((D_MODEL, tn), lambda j: (0, j)),
            pl.BlockSpec((1, tn), lambda j: (0, j)),
        ],
        out_specs=pl.BlockSpec((rows, tn), lambda j: (0, j)),
        out_shape=jax.ShapeDtypeStruct((rows, w.shape[1]), F32),
        compiler_params=_params("parallel"),
        name="ada_mod",
    )(cc, w, b.reshape(1, -1))


def _inproj_kernel(x_ref, sh_ref, sc_ref, w_ref, o_ref, xm_ref):
    @pl.when(pl.program_id(1) == 0)
    def _():
        xm_ref[...] = (x_ref[...] * (1.0 + sc_ref[...]) + sh_ref[...]).astype(BF16)

    o_ref[...] = _dot(xm_ref[...], w_ref[...])


def in_proj(x2, sh, sc, w_bf16, rows_per_group):
    n = x2.shape[0]
    tm = min(1024, rows_per_group)
    tn = 1024
    grp = lambda i, j: ((i * tm) // rows_per_group, 0, 0)
    return pl.pallas_call(
        _inproj_kernel,
        grid=(n // tm, IN_WIDTH // tn),
        in_specs=[
            pl.BlockSpec((tm, D_MODEL), lambda i, j: (i, 0)),
            pl.BlockSpec((None, 1, D_MODEL), grp),
            pl.BlockSpec((None, 1, D_MODEL), grp),
            pl.BlockSpec((D_MODEL, tn), lambda i, j: (0, j)),
        ],
        out_specs=pl.BlockSpec((tm, tn), lambda i, j: (i, j)),
        out_shape=jax.ShapeDtypeStruct((n, IN_WIDTH), F32),
        scratch_shapes=[pltpu.VMEM((tm, D_MODEL), BF16)],
        compiler_params=_params("parallel", "arbitrary"),
        name="in_proj",
    )(x2, sh, sc, w_bf16)


def _hgrn_constants(chunk, reverse):
    c = chunk
    levels = []
    m = c // 2
    while m >= 1:
        levels.append(m)
        m //= 2
    t = np.arange(c)[:, None]
    u = np.arange(c)[None, :]
    mats = [(u <= t), (u > t)]
    masks = [np.eye(c, dtype=bool)]
    for m in levels:
        blk = t // (2 * m)
        r = blk * 2 * m + m
        upper = (t % (2 * m)) >= m
        a = np.where(upper, (u > r) & (u <= t), (u > t) & (u <= r))
        mats.append(a)
        ts = np.arange(c)[:, None]
        ss = np.arange(c)[None, :]
        same = (ts // (2 * m)) == (ss // (2 * m))
        masks.append(same & ((ts % (2 * m)) >= m) & ((ss % (2 * m)) < m))
    amat = np.concatenate([np.asarray(a, np.float32) for a in mats], axis=0)
    msk = np.stack([np.asarray(a, np.float32) for a in masks], axis=0)
    if reverse:
        amat = amat.reshape(len(mats), c, c)[:, ::-1, ::-1].reshape(len(mats) * c, c)
        msk = msk[:, ::-1, ::-1]
    return jnp.asarray(amat, BF16), jnp.asarray(np.ascontiguousarray(msk), F32), len(levels)


def _hgrn_direction(q, z, v, lb, amat, mask_ref, st, chunk, n_levels, last_row):
    c = chunk
    f = lb + (1.0 - lb) * _sigmoid(z)
    lf = jnp.log(jnp.maximum(f, F32_TINY))
    kk = (1.0 - lb) * _sigmoid(-z)
    hi = lf.astype(BF16)
    lo = (lf - hi.astype(F32)).astype(BF16)
    g = _dot(amat, jnp.concatenate([hi, lo], axis=1))
    dall = g[:, :HG_D] + g[:, HG_D:]
    bq = dall[0:c]
    bk = dall[c:2 * c]
    vb = v.astype(BF16)
    scores = _dot_nt(q.astype(BF16), kk.astype(BF16)) * mask_ref[0]
    for m in range(n_levels):
        e = jnp.exp(dall[(2 + m) * c:(3 + m) * c])
        scores = scores + _dot_nt((q * e).astype(BF16), (kk * e).astype(BF16)) * mask_ref[m + 1]
    qd = (q * jnp.exp(bq)).astype(BF16)
    kd = (kk * jnp.exp(bk)).astype(BF16)
    o = _dot(scores.astype(BF16), vb) + _dot_nt(qd, st.astype(BF16))
    st_new = st * jnp.exp(bq[last_row:last_row + 1]) + _dot_tn(vb, kd)
    return o, st_new


def _hgrn_kernel(qf_ref, zf_ref, vf_ref, qb_ref, zb_ref, vb_ref, lbf_ref, lbb_ref,
                 af_ref, mf_ref, ab_ref, mb_ref, s0f_ref, s0b_ref,
                 of_ref, ob_ref, sf_ref, sb_ref, *, chunk, n_levels, heads):
    @pl.when(pl.program_id(2) == 0)
    def _():
        sf_ref[...] = s0f_ref[...]
        sb_ref[...] = s0b_ref[...]

    af = af_ref[...]
    ab = ab_ref[...]
    for h in range(heads):
        sl = slice(h * HG_D, (h + 1) * HG_D)
        o, st = _hgrn_direction(_silu(qf_ref[:, sl]), zf_ref[:, sl], vf_ref[:, sl], lbf_ref[:, sl],
                                af, mf_ref, sf_ref[h], chunk, n_levels, chunk - 1)
        of_ref[:, sl] = o
        sf_ref[h] = st
        o, st = _hgrn_direction(_silu(qb_ref[:, sl]), zb_ref[:, sl], vb_ref[:, sl], lbb_ref[:, sl],
                                ab, mb_ref, sb_ref[h], chunk, n_levels, 0)
        ob_ref[:, sl] = o
        sb_ref[h] = st


def hgrn_scan(z, lb_f, lb_b, s0f, s0b, batch, seq, chunk=128, heads=8):
    n = batch * seq
    nb = seq // chunk
    hw = heads * HG_D
    cpb = D_MODEL // hw
    af, mf, n_levels = _hgrn_constants(chunk, False)
    ab, mb, _ = _hgrn_constants(chunk, True)
    fwd = lambda seg: pl.BlockSpec((chunk, hw), lambda b, h, c: (b * nb + c, seg * cpb + h))
    bwd = lambda seg: pl.BlockSpec((chunk, hw), lambda b, h, c: (b * nb + nb - 1 - c, seg * cpb + h))
    const2 = lambda a: pl.BlockSpec(a.shape, lambda b, h, c: (0, 0))
    const3 = lambda a: pl.BlockSpec(a.shape, lambda b, h, c: (0, 0, 0))
    state = pl.BlockSpec((None, heads, HG_D, HG_D), lambda b, h, c: (b, h, 0, 0))
    lbspec = pl.BlockSpec((1, hw), lambda b, h, c: (0, h))
    kern = functools.partial(_hgrn_kernel, chunk=chunk, n_levels=n_levels, heads=heads)
    return pl.pallas_call(
        kern,
        grid=(batch, HG_HEADS // heads, nb),
        in_specs=[fwd(COL_HG_Q), fwd(COL_HG_FF), fwd(COL_HG_I),
                  bwd(COL_HG_Q), bwd(COL_HG_FB), bwd(COL_HG_I),
                  lbspec, lbspec, const2(af), const3(mf), const2(ab), const3(mb), state, state],
        out_specs=[pl.BlockSpec((chunk, hw), lambda b, h, c: (b * nb + c, h)),
                   pl.BlockSpec((chunk, hw), lambda b, h, c: (b * nb + nb - 1 - c, h)),
                   state, state],
        out_shape=[jax.ShapeDtypeStruct((n, D_MODEL), F32), jax.ShapeDtypeStruct((n, D_MODEL), F32),
                   jax.ShapeDtypeStruct(s0f.shape, F32), jax.ShapeDtypeStruct(s0b.shape, F32)],
        compiler_params=_params("parallel", "parallel", "arbitrary"),
        name="hgrn_scan",
    )(z, z, z, z, z, z, lb_f, lb_b, af, mf, ab, mb, s0f, s0b)


CONV_HALO = 16


def _conv_kernel(a_ref, g_ref, ap_ref, gp_ref, an_ref, gn_ref, dw_ref, db_ref, lg_ref, lb_ref,
                 o_ref, u_ref, s_ref, *, tl):
    i = pl.program_id(1)
    nblk = pl.num_programs(1)
    glu = lambda a, g: a * _sigmoid(g)
    prev_ok = (i > 0).astype(F32)
    next_ok = (i < nblk - 1).astype(F32)
    u_ref[0:CONV_HALO, :] = glu(ap_ref[...], gp_ref[...]) * prev_ok
    u_ref[CONV_HALO:CONV_HALO + tl, :] = glu(a_ref[...], g_ref[...])
    u_ref[CONV_HALO + tl:, :] = glu(an_ref[...], gn_ref[...]) * next_ok
    off = CONV_HALO - CONV_K // 2
    acc = jnp.zeros((tl, D_MODEL), F32) + db_ref[...]
    span = s_ref.shape[0] - tl
    for r in range(SUBLANES):
        s_ref[...] = u_ref[r:r + tl + span, :]
        for q in range(span // SUBLANES + 1):
            k = r + SUBLANES * q - off
            if 0 <= k < CONV_K:
                acc = acc + s_ref[SUBLANES * q:SUBLANES * q + tl, :] * dw_ref[k:k + 1, :]
    y = _layer_norm(acc, lg_ref[...], lb_ref[...])
    o_ref[...] = _silu(y).astype(BF16)


def conformer_conv(z, dw, db, ln_g, ln_b, batch, seq):
    n = batch * seq
    tl = min(256, seq)
    nblk = seq // tl
    hb = tl // CONV_HALO
    hpb = seq // CONV_HALO
    cur = lambda col: pl.BlockSpec((tl, D_MODEL), lambda b, i: (b * nblk + i, col))
    prev = lambda col: pl.BlockSpec(
        (CONV_HALO, D_MODEL), lambda b, i: (b * hpb + jnp.maximum(i * hb - 1, 0), col))
    nxt = lambda col: pl.BlockSpec(
        (CONV_HALO, D_MODEL), lambda b, i: (b * hpb + jnp.minimum((i + 1) * hb, hpb - 1), col))
    vec = lambda r: pl.BlockSpec((r, D_MODEL), lambda b, i: (0, 0))
    return pl.pallas_call(
        functools.partial(_conv_kernel, tl=tl),
        grid=(batch, nblk),
        in_specs=[cur(COL_CONV_A), cur(COL_CONV_G), prev(COL_CONV_A), prev(COL_CONV_G),
                  nxt(COL_CONV_A), nxt(COL_CONV_G), vec(CONV_K), vec(1), vec(1), vec(1)],
        out_specs=pl.BlockSpec((tl, D_MODEL), lambda b, i: (b * nblk + i, 0)),
        out_shape=jax.ShapeDtypeStruct((n, D_MODEL), BF16),
        scratch_shapes=[pltpu.VMEM((tl + 2 * CONV_HALO, D_MODEL), F32),
                        pltpu.VMEM((tl + 2 * CONV_HALO - SUBLANES, D_MODEL), F32)],
        compiler_params=_params("parallel", "parallel"),
        name="conformer_conv",
    )(z, z, z, z, z, z, dw, db.reshape(1, -1), ln_g.reshape(1, -1), ln_b.reshape(1, -1))


def _rope_tables(seq):
    n_rows = seq // GRID_W
    row = jnp.repeat(jnp.arange(n_rows), GRID_W).astype(F32)
    col = jnp.tile(jnp.arange(GRID_W), n_rows).astype(F32)
    n_freq = HEAD_DIM // 4
    inv = ROPE_THETA ** (-jnp.arange(n_freq, dtype=F32) / n_freq)
    ang = jnp.concatenate([row[:, None] * inv, col[:, None] * inv], axis=-1)
    cos = jnp.repeat(jnp.cos(ang), 2, axis=-1)
    sin = jnp.repeat(jnp.sin(ang), 2, axis=-1)
    sign = jnp.tile(jnp.asarray([-1.0, 1.0], F32), HEAD_DIM // 2)
    return cos, sin * sign


def _qkv_kernel(q_ref, kv_ref, qg_ref, kg_ref, cos_ref, sin_ref, qo_ref, ko_ref, vo_ref, *, rope):
    lane = lax.broadcasted_iota(jnp.int32, (1, HEAD_DIM), 1)
    even = (lane % 2) == 0

    def norm_rot(x, g):
        y = x * lax.rsqrt(jnp.mean(x * x, axis=-1, keepdims=True) + EPS) * g
        if rope:
            partner = jnp.where(even, pltpu.roll(y, HEAD_DIM - 1, 1), pltpu.roll(y, 1, 1))
            y = y * cos_ref[...] + partner * sin_ref[...]
        return y

    for h in range(ATT_HEADS):
        x = q_ref[:, h * HEAD_DIM:(h + 1) * HEAD_DIM]
        qo_ref[h] = (norm_rot(x, qg_ref[...]) * (HEAD_DIM ** -0.5)).astype(BF16)
    for h in range(ATT_KV_HEADS):
        x = kv_ref[:, h * HEAD_DIM:(h + 1) * HEAD_DIM]
        ko_ref[h] = norm_rot(x, kg_ref[...]).astype(BF16)
        off = (ATT_KV_HEADS + h) * HEAD_DIM
        vo_ref[h] = kv_ref[:, off:off + HEAD_DIM].astype(BF16)


def attn_qkv(z, qn_g, kn_g, cos, sin, batch, seq, rope):
    tl = min(512, seq)
    nblk = seq // tl
    tab = pl.BlockSpec((tl, HEAD_DIM), lambda b, i: (i, 0))
    vec = pl.BlockSpec((1, HEAD_DIM), lambda b, i: (0, 0))
    out = lambda nh: pl.BlockSpec((None, nh, tl, HEAD_DIM), lambda b, i: (b, 0, i, 0))
    return pl.pallas_call(
        functools.partial(_qkv_kernel, rope=rope),
        grid=(batch, nblk),
        in_specs=[pl.BlockSpec((tl, D_MODEL), lambda b, i: (b * nblk + i, COL_ATT_Q)),
                  pl.BlockSpec((tl, D_MODEL), lambda b, i: (b * nblk + i, COL_ATT_KV)),
                  vec, vec, tab, tab],
        out_specs=[out(ATT_HEADS), out(ATT_KV_HEADS), out(ATT_KV_HEADS)],
        out_shape=[jax.ShapeDtypeStruct((batch, ATT_HEADS, seq, HEAD_DIM), BF16),
                   jax.ShapeDtypeStruct((batch, ATT_KV_HEADS, seq, HEAD_DIM), BF16),
                   jax.ShapeDtypeStruct((batch, ATT_KV_HEADS, seq, HEAD_DIM), BF16)],
        compiler_params=_params("parallel", "parallel"),
        name="attn_qkv",
    )(z, z, qn_g.reshape(1, -1), kn_g.reshape(1, -1), cos, sin)


def _attn_kernel(q_ref, k_ref, v_ref, o_ref, *, tq):
    q = q_ref[...].reshape(ATT_GROUP * tq, HEAD_DIM)
    s = _dot_nt(q, k_ref[...])
    m = jnp.max(s, axis=-1, keepdims=True)
    p = jnp.exp(s - m)
    l = jnp.sum(p, axis=-1, keepdims=True)
    o = _dot(p.astype(BF16), v_ref[...]) / l
    for g in range(ATT_GROUP):
        o_ref[:, g * HEAD_DIM:(g + 1) * HEAD_DIM] = o[g * tq:(g + 1) * tq].astype(BF16)


def attention(q, k, v):
    batch, _, lq, _ = q.shape
    s_len = k.shape[2]
    tq = min(128, lq)
    nq = lq // tq
    kv = pl.BlockSpec((None, None, s_len, HEAD_DIM), lambda b, kh, i: (b, kh, 0, 0))
    return pl.pallas_call(
        functools.partial(_attn_kernel, tq=tq),
        grid=(batch, ATT_KV_HEADS, nq),
        in_specs=[pl.BlockSpec((None, ATT_GROUP, tq, HEAD_DIM), lambda b, kh, i: (b, kh, i, 0)), kv, kv],
        out_specs=pl.BlockSpec((tq, ATT_GROUP * HEAD_DIM), lambda b, kh, i: (b * nq + i, kh)),
        out_shape=jax.ShapeDtypeStruct((batch * lq, D_MODEL), BF16),
        compiler_params=_params("parallel", "parallel", "arbitrary"),
        name="attention",
    )(q, k, v)


def _merge_kernel(of_ref, ob_ref, hg_ref, uc_ref, at_ref, g0_ref, g1_ref, g2_ref, x_ref, xg_ref,
                  ng_ref, whg_ref, wcv_ref, wat_ref, wout_ref, lg_ref, lb_ref, o_ref, *, alpha):
    o = of_ref[...] + ob_ref[...]
    parts = []
    for h in range(HG_HEADS):
        oh = o[:, h * HG_D:(h + 1) * HG_D]
        parts.append(oh * lax.rsqrt(jnp.mean(oh * oh, axis=-1, keepdims=True) + EPS) * ng_ref[...])
    on = jnp.concatenate(parts, axis=1) * _silu(hg_ref[...])
    b_hg = _dot(on.astype(BF16), whg_ref[...])
    b_cv = _dot(uc_ref[...], wcv_ref[...])
    b_at = _dot(at_ref[...], wat_ref[...])
    mix_in = _sigmoid(g0_ref[...]) * b_hg + _sigmoid(g1_ref[...]) * b_cv + _sigmoid(g2_ref[...]) * b_at
    mix = _dot(mix_in.astype(BF16), wout_ref[...])
    y = alpha * x_ref[...] + xg_ref[...] * mix
    o_ref[...] = _layer_norm(y, lg_ref[...], lb_ref[...])


def merge_ln(o_f, o_b, z, uc, att, x2, xg, norm_g, w_hg, w_cv, w_at, w_out, ln_g, ln_b,
             rows_per_group, alpha):
    n = x2.shape[0]
    tm = min(256, rows_per_group)
    row = lambda col: pl.BlockSpec((tm, D_MODEL), lambda i: (i, col))
    wspec = pl.BlockSpec((D_MODEL, D_MODEL), lambda i: (0, 0))
    vec = lambda w: pl.BlockSpec((1, w), lambda i: (0, 0))
    grp = pl.BlockSpec((None, 1, D_MODEL), lambda i: ((i * tm) // rows_per_group, 0, 0))
    return pl.pallas_call(
        functools.partial(_merge_kernel, alpha=alpha),
        grid=(n // tm,),
        in_specs=[row(0), row(0), row(COL_HG_G), row(0), row(0),
                  row(COL_GATE0), row(COL_GATE0 + 1), row(COL_GATE0 + 2), row(0), grp,
                  vec(HG_D), wspec, wspec, wspec, wspec, vec(D_MODEL), vec(D_MODEL)],
        out_specs=row(0),
        out_shape=jax.ShapeDtypeStruct((n, D_MODEL), F32),
        compiler_params=_params("parallel"),
        name="merge_ln",
    )(o_f, o_b, z, uc, att, z, z, z, x2, xg, norm_g.reshape(1, -1),
      w_hg, w_cv, w_at, w_out, ln_g.reshape(1, -1), ln_b.reshape(1, -1))


def _peer_query_kernel(x_ref, sh_ref, sc_ref, wq_ref, k1_ref, k2_ref, t_ref, s1_ref, s2_ref):
    t = (x_ref[...] * (1.0 + sc_ref[...]) + sh_ref[...]).astype(BF16)
    t_ref[...] = t
    q = _dot(t, wq_ref[...]).astype(BF16)
    half = PEER_NKEYS
    for h in range(PEER_HEADS):
        s1_ref[h] = _dot_nt(k1_ref[...], q[:, (2 * h) * half:(2 * h + 1) * half])
        s2_ref[h] = _dot_nt(k2_ref[...], q[:, (2 * h + 1) * half:(2 * h + 2) * half])


def peer_query(x2, sh, sc, wq, k1, k2, rows_per_group):
    n = x2.shape[0]
    tm = min(512, rows_per_group)
    grp = pl.BlockSpec((None, 1, D_MODEL), lambda i: ((i * tm) // rows_per_group, 0, 0))
    sco = pl.BlockSpec((PEER_HEADS, PEER_NKEYS, tm), lambda i: (0, 0, i))
    return pl.pallas_call(
        _peer_query_kernel,
        grid=(n // tm,),
        in_specs=[pl.BlockSpec((tm, D_MODEL), lambda i: (i, 0)), grp, grp,
                  pl.BlockSpec(wq.shape, lambda i: (0, 0)),
                  pl.BlockSpec(k1.shape, lambda i: (0, 0)),
                  pl.BlockSpec(k2.shape, lambda i: (0, 0))],
        out_specs=[pl.BlockSpec((tm, D_MODEL), lambda i: (i, 0)), sco, sco],
        out_shape=[jax.ShapeDtypeStruct((n, D_MODEL), BF16),
                   jax.ShapeDtypeStruct((PEER_HEADS, PEER_NKEYS, n), F32),
                   jax.ShapeDtypeStruct((PEER_HEADS, PEER_NKEYS, n), F32)],
        compiler_params=_params("parallel"),
        name="peer_query",
    )(x2, sh, sc, wq, k1, k2)


_CAND_GROUPS = ((0, 16), (1, 8), (2, 5), (3, 4), (4, 3), (5, 2), (6, 2), (7, 2))
_CAND_ROWS = 16 + 8 * 7 + 8


def _sorted_top(s, out_ref):
    cur = s
    for r in range(PEER_TOPK):
        m = jnp.max(cur, axis=0, keepdims=True)
        out_ref[r:r + 1, :] = m
        cur = jnp.where(cur >= m, -jnp.inf, cur)
    return jnp.max(cur, axis=0, keepdims=True)


def _peer_route_kernel(s1_ref, s2_ref, thr_ref, e1_ref, e2_ref, v1_ref, v2_ref, cand_ref):
    s1 = s1_ref[...]
    s2 = s2_ref[...]
    v1_next = _sorted_top(s1, v1_ref)
    v2_next = _sorted_top(s2, v2_ref)
    tn = s1.shape[1]
    row8 = lax.broadcasted_iota(jnp.int32, (SUBLANES, tn), 0)
    cand_ref[0:16, :] = v1_ref[0:1, :] + v2_ref[...]
    base = 16
    for i, cnt in _CAND_GROUPS[1:]:
        c = v1_ref[i:i + 1, :] + v2_ref[0:SUBLANES, :]
        cand_ref[base:base + SUBLANES, :] = jnp.where(row8 < cnt, c, -jnp.inf)
        base += SUBLANES
    cand_ref[base:base + SUBLANES, :] = v1_ref[SUBLANES:2 * SUBLANES, :] + v2_ref[0:1, :]
    cand = cand_ref[...]
    cur = cand
    tau = None
    for r in range(PEER_TOPK):
        tau = jnp.max(cur, axis=0, keepdims=True)
        cur = jnp.where(cur >= tau, -jnp.inf, cur)
    m1 = v1_ref[0:1, :]
    m2 = v2_ref[0:1, :]
    zsum = jnp.sum(jnp.where(cand >= tau, jnp.exp(cand - (m1 + m2)), 0.0), axis=0, keepdims=True)
    runner_up = jnp.maximum(jnp.max(cur, axis=0, keepdims=True),
                            jnp.maximum(m1 + v2_next, v1_next + m2))
    cut = 0.5 * (tau + runner_up)
    thr = jnp.exp(cut - s1 - m2)
    e1 = jnp.exp(s1 - m1) * (0.5 / zsum)
    e2 = jnp.exp(s2 - m2)
    for p in range(tn // PEER_TOKEN_SUB):
        cols = slice(p * PEER_TOKEN_SUB, (p + 1) * PEER_TOKEN_SUB)
        thr_ref[p] = thr[:, cols]
        e1_ref[p] = e1[:, cols]
        e2_ref[p] = e2[:, cols]


PEER_TOKEN_SUB = 256


def peer_route(s1t, s2t):
    _, _, n = s1t.shape
    tn = min(512, n)
    blk = pl.BlockSpec((None, PEER_NKEYS, tn), lambda h, i: (h, 0, i))
    sub = pl.BlockSpec((None, tn // PEER_TOKEN_SUB, PEER_NKEYS, PEER_TOKEN_SUB), lambda h, i: (h, i, 0, 0))
    shp = jax.ShapeDtypeStruct((PEER_HEADS, n // PEER_TOKEN_SUB, PEER_NKEYS, PEER_TOKEN_SUB), F32)
    return pl.pallas_call(
        _peer_route_kernel,
        grid=(PEER_HEADS, n // tn),
        in_specs=[blk, blk],
        out_specs=[sub, sub, sub],
        out_shape=[shp, shp, shp],
        scratch_shapes=[pltpu.VMEM((PEER_TOPK, tn), F32), pltpu.VMEM((PEER_TOPK, tn), F32),
                        pltpu.VMEM((_CAND_ROWS, tn), F32)],
        compiler_params=_params("parallel", "parallel"),
        name="peer_route",
    )(s1t, s2t)


def _peer_mix_kernel(t_ref, u_ref, vt_ref, e2_ref, thr_ref, e1_ref, x_ref, xg_ref, lg_ref, lb_ref,
                     o_ref, acc_ref, act_ref, w_ref, *, n_sub, a_per_chunk, alpha):
    j = pl.program_id(1)

    @pl.when(j == 0)
    def _():
        acc_ref[...] = jnp.zeros_like(acc_ref)

    for p in range(n_sub):
        rows = slice(p * PEER_TOKEN_SUB, (p + 1) * PEER_TOKEN_SUB)
        act_ref[p] = _dot_nt(u_ref[...], t_ref[rows, :])

    c0 = math.sqrt(2.0 / math.pi)
    c1 = c0 * 0.044715

    def sub_block(p, carry):
        for lt in range(PEER_TOKEN_SUB // LANES):
            ls = slice(lt * LANES, (lt + 1) * LANES)
            for a in range(a_per_chunk):
                sel = jnp.zeros((PEER_NKEYS, LANES), F32)
                for h in range(PEER_HEADS):
                    e2 = e2_ref[h, p, :, ls]
                    keep = e2 >= thr_ref[h, p, a:a + 1, ls]
                    sel = sel + jnp.where(keep, e2, 0.0) * e1_ref[h, p, a:a + 1, ls]
                rs = slice(a * PEER_NKEYS, (a + 1) * PEER_NKEYS)
                xa = act_ref[p, rs, ls]
                th = jnp.tanh(xa * (c0 + c1 * (xa * xa)))
                w_ref[p, rs, ls] = ((xa + xa * th) * sel).astype(BF16)
        return carry

    lax.fori_loop(0, n_sub, sub_block, 0)

    for p in range(n_sub):
        acc_ref[p] += _dot(vt_ref[...], w_ref[p])

    @pl.when(j == pl.num_programs(1) - 1)
    def _():
        for p in range(n_sub):
            rows = slice(p * PEER_TOKEN_SUB, (p + 1) * PEER_TOKEN_SUB)
            y = alpha * x_ref[rows, :] + xg_ref[...] * acc_ref[p].T
            o_ref[rows, :] = _layer_norm(y, lg_ref[...], lb_ref[...])


def peer_mix(t, u_bf16, vt_bf16, e2s, thrt, e1t, x2, xg, ln_g, ln_b, rows_per_group, alpha):
    n = x2.shape[0]
    tm = min(512, rows_per_group)
    n_sub = tm // PEER_TOKEN_SUB
    a_per_chunk = 8
    ec = a_per_chunk * PEER_NKEYS
    n_chunks = u_bf16.shape[0] // ec
    part = pl.BlockSpec((PEER_HEADS, n_sub, a_per_chunk, PEER_TOKEN_SUB), lambda i, j: (0, i, j, 0))
    vec = pl.BlockSpec((1, D_MODEL), lambda i, j: (0, 0))
    grp = pl.BlockSpec((None, 1, D_MODEL), lambda i, j: ((i * tm) // rows_per_group, 0, 0))
    row = pl.BlockSpec((tm, D_MODEL), lambda i, j: (i, 0))
    buf = lambda dt: pltpu.VMEM((n_sub, ec, PEER_TOKEN_SUB), dt)
    return pl.pallas_call(
        functools.partial(_peer_mix_kernel, n_sub=n_sub, a_per_chunk=a_per_chunk, alpha=alpha),
        grid=(n // tm, n_chunks),
        in_specs=[row,
                  pl.BlockSpec((ec, D_MODEL), lambda i, j: (j, 0)),
                  pl.BlockSpec((D_MODEL, ec), lambda i, j: (0, j)),
                  pl.BlockSpec((PEER_HEADS, n_sub, PEER_NKEYS, PEER_TOKEN_SUB), lambda i, j: (0, i, 0, 0)),
                  part, part, row, grp, vec, vec],
        out_specs=row,
        out_shape=jax.ShapeDtypeStruct((n, D_MODEL), F32),
        scratch_shapes=[pltpu.VMEM((n_sub, D_MODEL, PEER_TOKEN_SUB), F32), buf(F32), buf(BF16)],
        compiler_params=_params("parallel", "arbitrary"),
        name="peer_mix",
    )(t, u_bf16, vt_bf16, e2s, thrt, e1t, x2, xg, ln_g.reshape(1, -1), ln_b.reshape(1, -1))


def kernel(x, c, ctx, c_ctx, w_ada, b_ada, w_in, hg_lb_logits, hg_norm_g, w_hg_o, conv_dw, conv_b,
           conv_ln_g, conv_ln_b, w_conv_o, att_qn_g, att_kn_g, w_att_o, w_out, ln1_g, ln1_b,
           peer_wq, peer_k1, peer_k2, peer_u, peer_v, ln2_g, ln2_b):
    batch, seq, _ = x.shape
    ctx_len = ctx.shape[1]
    depth = w_ada.shape[0]
    alpha = (2 * depth) ** 0.25
    n_x = batch * seq
    n_c = batch * ctx_len

    lb_p = jax.nn.softmax(hg_lb_logits.astype(F32), axis=0)
    lb = jnp.cumsum(lb_p, axis=0) - lb_p
    cos, sin = _rope_tables(seq)
    cc = jnp.concatenate([c, c_ctx[None, :], jnp.zeros((SUBLANES - 1 - batch, D_MODEL), F32)], axis=0)

    x2 = x.reshape(n_x, D_MODEL)
    c2 = ctx.reshape(n_c, D_MODEL)
    zero_state = jnp.zeros((batch, HG_HEADS, HG_D, HG_D), F32)

    for l in range(depth):
        last = l == depth - 1
        mod = ada_mod(cc, w_ada[l], b_ada[l])
        mx = mod[:batch].reshape(batch, 1, N_ADA, D_MODEL)
        mc = mod[batch:batch + 1].reshape(1, 1, N_ADA, D_MODEL)
        xsh1, xsc1, xg1, xsh2, xsc2, xg2 = [mx[:, :, k] for k in range(N_ADA)]
        csh1, csc1, cg1, csh2, csc2, cg2 = [mc[:, :, k] for k in range(N_ADA)]

        w_in_b = w_in[l].astype(BF16)
        zx = in_proj(x2, xsh1, xsc1, w_in_b, seq)
        zc = in_proj(c2, csh1, csc1, w_in_b, n_c)

        lb_f = lb[l, 0].reshape(1, -1)
        lb_b = lb[l, 1].reshape(1, -1)
        ocf, ocb, st_f, st_b = hgrn_scan(zc, lb_f, lb_b, zero_state, zero_state, batch, ctx_len)
        oxf, oxb, _, _ = hgrn_scan(zx, lb_f, lb_b, st_f, st_b, batch, seq)

        qx, kx, vx = attn_qkv(zx, att_qn_g[l], att_kn_g[l], cos, sin, batch, seq, True)
        qc, kc, vc = attn_qkv(zc, att_qn_g[l], att_kn_g[l], cos, sin, batch, ctx_len, False)
        att_x = attention(qx, jnp.concatenate([kx, kc], axis=2), jnp.concatenate([vx, vc], axis=2))
        ucx = conformer_conv(zx, conv_dw[l], conv_b[l], conv_ln_g[l], conv_ln_b[l], batch, seq)

        w_hg, w_cv, w_at, w_o = (w.astype(BF16) for w in (w_hg_o[l], w_conv_o[l], w_att_o[l], w_out[l]))
        wq = peer_wq[l].astype(BF16)
        k1 = peer_k1[l].astype(BF16)
        k2 = peer_k2[l].astype(BF16)
        u_b = peer_u[l].astype(BF16)
        vt_b = peer_v[l].astype(BF16).T

        def channel_mix(xs, sh2, sc2, g2, rows_per_group):
            t, s1t, s2t = peer_query(xs, sh2, sc2, wq, k1, k2, rows_per_group)
            thr, e1, e2 = peer_route(s1t, s2t)
            return peer_mix(t, u_b, vt_b, e2, thr, e1, xs, g2, ln2_g[l], ln2_b[l],
                            rows_per_group, alpha)

        x2 = merge_ln(oxf, oxb, zx, ucx, att_x, x2, xg1, hg_norm_g[l], w_hg, w_cv, w_at, w_o,
                      ln1_g[l], ln1_b[l], seq, alpha)
        x2 = channel_mix(x2, xsh2, xsc2, xg2, seq)

        if not last:
            att_c = attention(qc, kc, vc)
            ucc = conformer_conv(zc, conv_dw[l], conv_b[l], conv_ln_g[l], conv_ln_b[l], batch, ctx_len)
            c2 = merge_ln(ocf, ocb, zc, ucc, att_c, c2, cg1, hg_norm_g[l], w_hg, w_cv, w_at, w_o,
                          ln1_g[l], ln1_b[l], n_c, alpha)
            c2 = channel_mix(c2, csh2, csc2, cg2, n_c)

    return x2.reshape(batch, seq, D_MODEL)
```
